```python
import math
import jax
import jax.numpy as jnp
from jax import lax
import numpy as np

D_MODEL = 4096
BATCH = 4
SEQ = 2048
DEPTH = 4
DEC_BATCH = 128
DEC_SEQ = 1
PAST_LEN = 8192
PAGE_SIZE = 128

N_MIXERS = 4
N_HEADS = 32
HEAD_DIM = 128
D_FF = -(-8 * D_MODEL // (3 * 256)) * 256
QBLOCK = 128
RMS_EPS = 1e-6
REL_BUCKETS = 32
REL_MAX_DIST = 128
REL_COLS = 32
NSA_KV = 1
NSA_BLOCK = 64
NSA_TOPK = 16
NSA_WINDOW = 512
MLA_Q_RANK = 1024
MLA_KV_RANK = 512
MLA_NOPE = 128
MLA_ROPE = 64
MLA_V = 128
ROPE_BASE = 10000.0
DIFF_HEADS = 16
DIFF_D = 128
DIFF_KV = 1
DIFF_LAYER = 3
DIFF_LAMBDA_INIT = 0.8 - 0.6 * math.exp(-0.3 * (DIFF_LAYER - 1))
SB_KV = 2

kernel_name = 'hybrid_nsa_mla_diff_stickbreak_decode_step'


def rmsnorm(x, g):
    xf = x.astype(jnp.float32)
    y = xf * lax.rsqrt(jnp.mean(xf * xf, axis=-1, keepdims=True) + RMS_EPS)
    return (y * g.astype(jnp.float32)).astype(x.dtype)


def rope(x, pos):
    half = x.shape[-1] // 2
    inv_freq = jnp.exp(-math.log(ROPE_BASE) * jnp.arange(half, dtype=jnp.float32) / half)
    ang = pos.astype(jnp.float32)[:, None] * inv_freq[None, :]
    ang = ang.reshape((1, pos.shape[0]) + (1,) * (x.ndim - 3) + (half,))
    cos, sin = jnp.cos(ang), jnp.sin(ang)
    xf = x.astype(jnp.float32)
    x1, x2 = xf[..., :half], xf[..., half:]
    return jnp.concatenate([x1 * cos - x2 * sin, x2 * cos + x1 * sin], axis=-1).astype(x.dtype)


def rel_bucket(dist):
    max_exact = REL_BUCKETS // 2
    d = jnp.maximum(dist, 0)
    log_ratio = jnp.log(jnp.maximum(d, 1).astype(jnp.float32) / max_exact) / math.log(REL_MAX_DIST / max_exact)
    far = max_exact + (log_ratio * (REL_BUCKETS - max_exact)).astype(jnp.int32)
    return jnp.where(d < max_exact, d, jnp.minimum(far, REL_BUCKETS - 1))


def head_bias(table, dist, n_groups):
    b = table[rel_bucket(dist)].astype(jnp.float32)
    return jnp.moveaxis(b, -1, 0).reshape((n_groups, -1) + dist.shape)


def grouped_head_bias(table, dist, n_groups):
    tg = table.T.reshape(n_groups, -1, REL_BUCKETS)
    b = tg[jnp.arange(n_groups)[None, :, None, None], :, rel_bucket(dist)]
    return jnp.moveaxis(b, -1, 2).astype(jnp.float32)


def masked_softmax(logits, mask):
    p = jax.nn.softmax(jnp.where(mask, logits, -1e30), axis=-1)
    return jnp.where(mask, p, 0.0)


def gather_pages(pool, page_table):
    g = pool[page_table]
    return g.reshape((g.shape[0], g.shape[1] * g.shape[2]) + g.shape[3:])


def gather_blocks(blocks_g, idx):
    out = jax.vmap(jax.vmap(lambda kb, ix: kb[ix]))(blocks_g, idx)
    return out.reshape(idx.shape[:3] + (-1, blocks_g.shape[-1]))


def sweep_queries(fn, q_args, q_pos):
    Tq = q_pos.shape[0]
    if Tq <= QBLOCK or Tq % QBLOCK:
        return fn(*q_args, q_pos)
    nb = Tq // QBLOCK
    split = lambda a: jnp.moveaxis(a.reshape((a.shape[0], nb, QBLOCK) + a.shape[2:]), 1, 0)
    blocks = tuple(split(a) for a in q_args) + (q_pos.reshape(nb, QBLOCK),)
    out = lax.map(lambda args: fn(*args), blocks)
    return jnp.moveaxis(out, 0, 1).reshape((out.shape[1], Tq) + out.shape[3:])


def swiglu(h, w_in, w_out):
    gu = h @ w_in
    return (jax.nn.silu(gu[..., :D_FF]) * gu[..., D_FF:]) @ w_out


def nsa_mixer(h, pos, past_kv, win_buf, rel_table, w_in, w_o):
    B, T, _ = h.shape
    dt = h.dtype
    G, R, L = NSA_KV, N_HEADS // NSA_KV, NSA_BLOCK
    nq, nkv = N_HEADS * HEAD_DIM, 6 * NSA_KV * HEAD_DIM
    proj = h @ w_in
    q = proj[..., :nq].reshape(B, T, G, R, HEAD_DIM)
    kv_new = proj[..., nq:nq + nkv].reshape(B, T, 6, G, HEAD_DIM)
    gates = jax.nn.sigmoid(proj[..., nq + nkv:].astype(jnp.float32)).astype(dt).reshape(B, T, 3, G, R)
    full_new, win_new = kv_new[:, :, :4], kv_new[:, :, 4:]
    full = full_new if past_kv is None else jnp.concatenate([past_kv, full_new], axis=1)
    Tk = full.shape[1]
    NB = -(-Tk // L)
    full = jnp.pad(full, ((0, 0), (0, NB * L - Tk), (0, 0), (0, 0), (0, 0)))
    blocks = full.reshape(B, NB, L, 4, G, HEAD_DIM)
    k_cmp = jnp.mean(blocks[:, :, :, 0], axis=2)
    v_cmp = jnp.mean(blocks[:, :, :, 1], axis=2)
    k_sel = jnp.moveaxis(blocks[:, :, :, 2], 3, 1)
    v_sel = jnp.moveaxis(blocks[:, :, :, 3], 3, 1)
    blk = jnp.arange(NB, dtype=jnp.int32)
    blk_end = blk * L + (L - 1)
    topk = min(NSA_TOPK, NB)
    if win_buf is None:
        win_pad = jnp.pad(win_new, ((0, 0), (NSA_WINDOW, 0), (0, 0), (0, 0), (0, 0)))

        def window_rows(qp):
            n = NSA_WINDOW + qp.shape[0]
            rows = lax.dynamic_slice_in_dim(win_pad, qp[0], n, axis=1)
            return rows, qp[0] - NSA_WINDOW + jnp.arange(n, dtype=jnp.int32)
        new_win = win_new[:, -NSA_WINDOW:]
    else:
        win_all = jnp.concatenate([win_buf, win_new], axis=1)
        win_pos = pos[0] - win_buf.shape[1] + jnp.arange(win_all.shape[1], dtype=jnp.int32)

        def window_rows(qp):
            return win_all, win_pos
        new_win = win_all[:, -win_buf.shape[1]:]
    scale = HEAD_DIM ** -0.5

    def attend(q_b, g_b, qp):
        Tb = qp.shape[0]
        dist_c = qp[:, None] - blk_end[None, :]
        s_c = jnp.einsum('bqgrd,bngd->bgrqn', q_b, k_cmp).astype(jnp.float32) * scale + head_bias(rel_table, dist_c, G)
        p_c = masked_softmax(s_c, dist_c >= 0)
        o_c = jnp.einsum('bgrqn,bngd->bqgrd', p_c.astype(dt), v_cmp)
        cur = qp[:, None] // L
        forced = (blk[None, :] == 0) | (blk[None, :] == cur) | (blk[None, :] == cur - 1)
        imp = jnp.where(forced, 2.0, jnp.sum(p_c, axis=2))
        imp = jnp.where(blk[None, :] <= cur, imp, -1.0)
        top_val, top_idx = lax.top_k(imp, topk)
        ks = gather_blocks(k_sel, top_idx)
        vs = gather_blocks(v_sel, top_idx)
        kpos = (top_idx[..., None] * L + jnp.arange(L, dtype=jnp.int32)).reshape(B, G, Tb, topk * L)
        dist_s = qp[None, None, :, None] - kpos
        mask_s = jnp.repeat(top_val >= 0.0, L, axis=-1) & (dist_s >= 0)
        s_s = jnp.einsum('bqgrd,bgqkd->bgrqk', q_b, ks).astype(jnp.float32) * scale + grouped_head_bias(rel_table, dist_s, G)
        p_s = masked_softmax(s_s, mask_s[:, :, None])
        o_s = jnp.einsum('bgrqk,bgqkd->bqgrd', p_s.astype(dt), vs)
        rows_w, kpos_w = window_rows(qp)
        dist_w = qp[:, None] - kpos_w[None, :]
        mask_w = (dist_w >= 0) & (dist_w < NSA_WINDOW) & (kpos_w[None, :] >= 0)
        s_w = jnp.einsum('bqgrd,bkgd->bgrqk', q_b, rows_w[:, :, 0]).astype(jnp.float32) * scale + head_bias(rel_table, dist_w, G)
        p_w = masked_softmax(s_w, mask_w)
        o_w = jnp.einsum('bgrqk,bkgd->bqgrd', p_w.astype(dt), rows_w[:, :, 1])
        g_b = g_b[..., None]
        o = g_b[:, :, 0] * o_c + g_b[:, :, 1] * o_s + g_b[:, :, 2] * o_w
        return o.reshape(B, Tb, N_HEADS * HEAD_DIM)

    o = sweep_queries(attend, (q, gates), pos)
    return o @ w_o, full_new, new_win


def mla_mixer(h, pos, past, g_q, g_kv, w_in, w_uq, w_ukv, w_o):
    B, T, _ = h.shape
    dt = h.dtype
    proj = h @ w_in
    c_q = rmsnorm(proj[..., :MLA_Q_RANK], g_q)
    c_kv = rmsnorm(proj[..., MLA_Q_RANK:MLA_Q_RANK + MLA_KV_RANK], g_kv)
    k_pe = rope(proj[..., MLA_Q_RANK + MLA_KV_RANK:], pos)
    q = (c_q @ w_uq).reshape(B, T, N_HEADS, MLA_NOPE + MLA_ROPE)
    q_pe = rope(q[..., MLA_NOPE:], pos)
    q_lat = jnp.einsum('bthn,chn->bthc', q[..., :MLA_NOPE], w_ukv[:, :, :MLA_NOPE])
    rows_new = jnp.concatenate([c_kv, k_pe], axis=-1)
    rows = rows_new if past is None else jnp.concatenate([past, rows_new], axis=1)
    ckv, kpe = rows[..., :MLA_KV_RANK], rows[..., MLA_KV_RANK:]
    kpos = jnp.arange(rows.shape[1], dtype=jnp.int32)
    scale = (MLA_NOPE + MLA_ROPE) ** -0.5

    def attend(ql, qpe, qp):
        s = jnp.einsum('bqhc,bkc->bhqk', ql, ckv) + jnp.einsum('bqhr,bkr->bhqk', qpe, kpe)
        p = masked_softmax(s.astype(jnp.float32) * scale, kpos[None, :] <= qp[:, None])
        o_lat = jnp.einsum('bhqk,bkc->bqhc', p.astype(dt), ckv)
        o = jnp.einsum('bqhc,chv->bqhv', o_lat, w_ukv[:, :, MLA_NOPE:])
        return o.reshape(B, qp.shape[0], N_HEADS * MLA_V)

    o = sweep_queries(attend, (q_lat, q_pe), pos)
    return o @ w_o, rows_new


def diff_mixer(h, pos, past, rel_table, w_in, lam_p, g_sub, w_o):
    B, T, _ = h.shape
    dt = h.dtype
    G, R, d = DIFF_KV, DIFF_HEADS // DIFF_KV, DIFF_D
    nq = DIFF_HEADS * 2 * d
    proj = h @ w_in
    q = proj[..., :nq].reshape(B, T, G, R, 2, d)
    rows_new = proj[..., nq:].reshape(B, T, 2, G, 2 * d)
    rows = rows_new if past is None else jnp.concatenate([past, rows_new], axis=1)
    Tk = rows.shape[1]
    k = rows[:, :, 0].reshape(B, Tk, G, 2, d)
    v = rows[:, :, 1]
    kpos = jnp.arange(Tk, dtype=jnp.int32)
    lp = lam_p.astype(jnp.float32)
    lam = jnp.exp(jnp.sum(lp[0] * lp[1])) - jnp.exp(jnp.sum(lp[2] * lp[3])) + DIFF_LAMBDA_INIT
    scale = d ** -0.5

    def attend(q_b, qp):
        Tb = qp.shape[0]
        dist = qp[:, None] - kpos[None, :]
        bias = rel_table[rel_bucket(dist)].astype(jnp.float32).reshape(Tb, Tk, G, R, 2)
        bias = jnp.transpose(bias, (4, 2, 3, 0, 1))[:, None]
        s = jnp.einsum('bqgrid,bkgid->ibgrqk', q_b, k).astype(jnp.float32) * scale + bias
        p = masked_softmax(s, dist >= 0)
        a = p[0] - lam * p[1]
        o = jnp.einsum('bgrqk,bkge->bqgre', a.astype(dt), v)
        o = rmsnorm(o, g_sub) * (1.0 - DIFF_LAMBDA_INIT)
        return o.reshape(B, Tb, nq)

    o = sweep_queries(attend, (q,), pos)
    return o @ w_o, rows_new


def sb_mixer(h, pos, past, w_in, w_o):
    B, T, _ = h.shape
    dt = h.dtype
    G, R = SB_KV, N_HEADS // SB_KV
    nq = N_HEADS * HEAD_DIM
    proj = h @ w_in
    q = proj[..., :nq].reshape(B, T, G, R, HEAD_DIM)
    rows_new = proj[..., nq:].reshape(B, T, 2, G, HEAD_DIM)
    rows = rows_new if past is None else jnp.concatenate([past, rows_new], axis=1)
    k, v = rows[:, :, 0], rows[:, :, 1]
    kpos = jnp.arange(rows.shape[1], dtype=jnp.int32)
    scale = HEAD_DIM ** -0.5

    def attend(q_b, qp):
        z = jnp.einsum('bqgrd,bkgd->bgrqk', q_b, k).astype(jnp.float32) * scale
        mask = kpos[None, :] < qp[:, None]
        log_keep = jnp.where(mask, jax.nn.log_sigmoid(-z), 0.0)
        later = lax.cumsum(log_keep, axis=z.ndim - 1, reverse=True) - log_keep
        a = jnp.where(mask, jnp.exp(jax.nn.log_sigmoid(z) + later), 0.0)
        o = jnp.einsum('bgrqk,bkgd->bqgrd', a.astype(dt), v)
        return o.reshape(B, qp.shape[0], nq)

    o = sweep_queries(attend, (q,), pos)
    return o @ w_o, rows_new


def setup_inputs(seed: int = 0) -> dict:
    key = jax.random.key(seed)
    keys = iter(jax.random.split(key, 32))
    f32 = jnp.float32

    def normal(shape, scale=1.0):
        return jax.random.normal(next(keys), shape, f32) * scale

    def gain(shape):
        return 1.0 + normal(shape, 0.01)

    n_pages = PAST_LEN // PAGE_SIZE
    n_used = DEC_BATCH * n_pages
    n_pool = n_used + n_used // 4
    win_buf = min(NSA_WINDOW, PAST_LEN)
    page_table = jax.random.permutation(next(keys), n_pool)[:n_used].reshape(DEC_BATCH, n_pages).astype(jnp.int32)
    D = D_MODEL
    return {
        'x_prompt': normal((BATCH, SEQ, D)),
        'x_sample': normal((DEC_BATCH, DEC_SEQ, D)),
        'cache_nsa_kv': normal((n_pool, PAGE_SIZE, 4, NSA_KV, HEAD_DIM)),
        'state_nsa_win': normal((DEC_BATCH, win_buf, 2, NSA_KV, HEAD_DIM)),
        'cache_mla': normal((n_pool, PAGE_SIZE, MLA_KV_RANK + MLA_ROPE)),
        'cache_diff_kv': normal((n_pool, PAGE_SIZE, 2, DIFF_KV, 2 * DIFF_D)),
        'cache_sb_kv': normal((n_pool, PAGE_SIZE, 2, SB_KV, HEAD_DIM)),
        'page_table': page_table,
        'rel_table': normal((REL_BUCKETS, REL_COLS), 0.1),
        'norm_g': gain((DEPTH, 2, D)),
        'final_g': gain((D,)),
        'nsa_w_in': normal((D, N_HEADS * HEAD_DIM + 6 * NSA_KV * HEAD_DIM + 3 * N_HEADS), D ** -0.5),
        'nsa_w_o': normal((N_HEADS * HEAD_DIM, D), (N_HEADS * HEAD_DIM) ** -0.5),
        'mla_w_in': normal((D, MLA_Q_RANK + MLA_KV_RANK + MLA_ROPE), D ** -0.5),
        'mla_g_q': gain((MLA_Q_RANK,)),
        'mla_g_kv': gain((MLA_KV_RANK,)),
        'mla_w_uq': normal((MLA_Q_RANK, N_HEADS * (MLA_NOPE + MLA_ROPE)), MLA_Q_RANK ** -0.5),
        'mla_w_ukv': normal((MLA_KV_RANK, N_HEADS, MLA_NOPE + MLA_V), MLA_KV_RANK ** -0.5),
        'mla_w_o': normal((N_HEADS * MLA_V, D), (N_HEADS * MLA_V) ** -0.5),
        'diff_w_in': normal((D, DIFF_HEADS * 2 * DIFF_D + 2 * DIFF_KV * 2 * DIFF_D), D ** -0.5),
        'diff_lambda': normal((4, DIFF_D), 0.1),
        'diff_g_sub': gain((2 * DIFF_D,)),
        'diff_w_o': normal((DIFF_HEADS * 2 * DIFF_D, D), (DIFF_HEADS * 2 * DIFF_D) ** -0.5),
        'sb_w_in': normal((D, N_HEADS * HEAD_DIM + 2 * SB_KV * HEAD_DIM), D ** -0.5),
        'sb_w_o': normal((N_HEADS * HEAD_DIM, D), (N_HEADS * HEAD_DIM) ** -0.5),
        'ffn_w_in': normal((DEPTH, D, 2 * D_FF), D ** -0.5),
        'ffn_w_out': normal((DEPTH, D_FF, D), D_FF ** -0.5),
    }


def reference(x_prompt, x_sample, cache_nsa_kv, state_nsa_win, cache_mla, cache_diff_kv, cache_sb_kv, page_table,
              rel_table, norm_g, final_g, nsa_w_in, nsa_w_o, mla_w_in, mla_g_q, mla_g_kv, mla_w_uq, mla_w_ukv, mla_w_o,
              diff_w_in, diff_lambda, diff_g_sub, diff_w_o, sb_w_in, sb_w_o, ffn_w_in, ffn_w_out):
    pos_p = jnp.arange(x_prompt.shape[1], dtype=jnp.int32)
    pos_s = PAST_LEN + jnp.arange(x_sample.shape[1], dtype=jnp.int32)
    yp, ys = x_prompt, x_sample
    for i in range(DEPTH):
        kind = i % N_MIXERS
        hp = rmsnorm(yp, norm_g[i, 0])
        hs = rmsnorm(ys, norm_g[i, 0])
        if kind == 0:
            mp, nsa_kv_p, nsa_win_p = nsa_mixer(hp, pos_p, None, None, rel_table, nsa_w_in, nsa_w_o)
            ms, nsa_kv_s, nsa_win_s = nsa_mixer(hs, pos_s, gather_pages(cache_nsa_kv, page_table), state_nsa_win,
                                                rel_table, nsa_w_in, nsa_w_o)
        elif kind == 1:
            mp, mla_p = mla_mixer(hp, pos_p, None, mla_g_q, mla_g_kv, mla_w_in, mla_w_uq, mla_w_ukv, mla_w_o)
            ms, mla_s = mla_mixer(hs, pos_s, gather_pages(cache_mla, page_table), mla_g_q, mla_g_kv, mla_w_in,
                                  mla_w_uq, mla_w_ukv, mla_w_o)
        elif kind == 2:
            mp, diff_p = diff_mixer(hp, pos_p, None, rel_table, diff_w_in, diff_lambda, diff_g_sub, diff_w_o)
            ms, diff_s = diff_mixer(hs, pos_s, gather_pages(cache_diff_kv, page_table), rel_table, diff_w_in,
                                    diff_lambda, diff_g_sub, diff_w_o)
        else:
            mp, sb_p = sb_mixer(hp, pos_p, None, sb_w_in, sb_w_o)
            ms, sb_s = sb_mixer(hs, pos_s, gather_pages(cache_sb_kv, page_table), sb_w_in, sb_w_o)
        yp = yp + mp
        ys = ys + ms
        yp = yp + swiglu(rmsnorm(yp, norm_g[i, 1]), ffn_w_in[i], ffn_w_out[i])
        ys = ys + swiglu(rmsnorm(ys, norm_g[i, 1]), ffn_w_in[i], ffn_w_out[i])
    return (rmsnorm(yp, final_g), rmsnorm(ys, final_g), nsa_kv_p, nsa_kv_s, nsa_win_p, nsa_win_s,
            mla_p, mla_s, diff_p, diff_s, sb_p, sb_s)
```

```python
import functools
import math

import jax
import jax.numpy as jnp
from jax import lax
from jax.experimental import pallas as pl
from jax.experimental.pallas import tpu as pltpu

BF16 = jnp.bfloat16
F32 = jnp.float32

N_HEADS = 32
HEAD_DIM = 128
QBLOCK = 128
RMS_EPS = 1e-6
REL_BUCKETS = 32
REL_MAX_DIST = 128
NSA_BLOCK = 64
NSA_TOPK = 16
NSA_WINDOW = 512
MLA_NOPE = 128
MLA_ROPE = 64
MLA_V = 128
ROPE_BASE = 10000.0
DIFF_HEADS = 16
DIFF_D = 128
DIFF_LAMBDA_INIT = 0.8 - 0.6 * math.exp(-0.3 * 2)
SB_KV = 2

V7X_VMEM_BYTES = 64 * 1024 * 1024
V7X_LANES = 128
NEG_BIG = -1e30


def _cparams(semantics, vmem_mb):
    return pltpu.CompilerParams(dimension_semantics=semantics, vmem_limit_bytes=vmem_mb * 1024 * 1024)


def _tile(n, target, mult=16):
    best = None
    for t in range(mult, min(n, target) + 1, mult):
        if n % t == 0:
            best = t
    return best if best is not None else n


def _rms(x, g):
    y = x * lax.rsqrt(jnp.mean(x * x, axis=-1, keepdims=True) + RMS_EPS)
    return y * g


def _norm_kernel(x_ref, g_ref, h_ref):
    h_ref[...] = _rms(x_ref[...], g_ref[...]).astype(h_ref.dtype)


def _add_norm_kernel(x_ref, d_ref, g_ref, y_ref, h_ref):
    y = x_ref[...] + d_ref[...]
    y_ref[...] = y
    h_ref[...] = _rms(y, g_ref[...]).astype(h_ref.dtype)


def _add_final_kernel(x_ref, d_ref, g_ref, o_ref):
    o_ref[...] = _rms(x_ref[...] + d_ref[...], g_ref[...])


def rms_norm(x, g):
    m, d = x.shape
    tm = _tile(m, 320)
    row = pl.BlockSpec((tm, d), lambda i: (i, 0))
    return pl.pallas_call(
        _norm_kernel, grid=(m // tm,),
        in_specs=[row, pl.BlockSpec((1, d), lambda i: (0, 0))],
        out_specs=row, out_shape=jax.ShapeDtypeStruct((m, d), BF16),
        compiler_params=_cparams(("parallel",), 32), name="rms_norm",
    )(x, g.reshape(1, d))


def add_rms_norm(x, delta, g):
    m, d = x.shape
    tm = _tile(m, 320)
    row = pl.BlockSpec((tm, d), lambda i: (i, 0))
    return pl.pallas_call(
        _add_norm_kernel, grid=(m // tm,),
        in_specs=[row, row, pl.BlockSpec((1, d), lambda i: (0, 0))],
        out_specs=[row, row],
        out_shape=[jax.ShapeDtypeStruct((m, d), F32), jax.ShapeDtypeStruct((m, d), BF16)],
        compiler_params=_cparams(("parallel",), 48), name="add_rms_norm",
    )(x, delta, g.reshape(1, d))


def add_final_norm(x, delta, g):
    m, d = x.shape
    tm = _tile(m, 320)
    row = pl.BlockSpec((tm, d), lambda i: (i, 0))
    return pl.pallas_call(
        _add_final_kernel, grid=(m // tm,),
        in_specs=[row, row, pl.BlockSpec((1, d), lambda i: (0, 0))],
        out_specs=row, out_shape=jax.ShapeDtypeStruct((m, d), F32),
        compiler_params=_cparams(("parallel",), 48), name="add_final_norm",
    )(x, delta, g.reshape(1, d))


def _dense_kernel(*refs, epilogue, n_extra, n_out):
    a_ref, w_ref = refs[0], refs[1]
    extra = refs[2:2 + n_extra]
    outs = refs[2 + n_extra:2 + n_extra + n_out]
    acc = jnp.dot(a_ref[...], w_ref[...], preferred_element_type=F32)
    epilogue(acc, extra, outs)


def dense(a, w, epilogue, outs, extras=(), tn=512, tm_target=640, name="dense"):
    m, k = a.shape
    n = w.shape[1]
    tn = min(tn, n)
    assert n % tn == 0
    tm = _tile(m, tm_target)
    in_specs = [pl.BlockSpec((tm, k), lambda i, j: (i, 0)), pl.BlockSpec((k, tn), lambda i, j: (0, j))]
    args = [a, w]
    for arr, kind in extras:
        if kind == "tile":
            in_specs.append(pl.BlockSpec((tm, tn), lambda i, j: (i, j)))
        elif kind == "row":
            in_specs.append(pl.BlockSpec((tm, arr.shape[1]), lambda i, j: (i, 0)))
        else:
            in_specs.append(pl.BlockSpec(arr.shape, lambda i, j, nd=arr.ndim: (0,) * nd))
        args.append(arr)
    out_specs, out_shape = [], []
    for dtype, cols, bcols in outs:
        out_specs.append(pl.BlockSpec((tm, bcols), lambda i, j: (i, j)))
        out_shape.append(jax.ShapeDtypeStruct((m, cols), dtype))
    kern = functools.partial(_dense_kernel, epilogue=epilogue, n_extra=len(extras), n_out=len(outs))
    res = pl.pallas_call(
        kern, grid=(m // tm, n // tn), in_specs=in_specs, out_specs=out_specs, out_shape=out_shape,
        compiler_params=_cparams(("parallel", "arbitrary"), 48), name=name,
    )(*args)
    return res


def _epi_bf16(acc, extra, outs):
    outs[0][...] = acc.astype(BF16)


def _epi_f32_bf16(acc, extra, outs):
    outs[0][...] = acc
    outs[1][...] = acc.astype(BF16)


def _epi_sigmoid(acc, extra, outs):
    outs[0][...] = 1.0 / (1.0 + jnp.exp(-acc))


def _epi_residual(acc, extra, outs):
    outs[0][...] = extra[0][...] + acc


def _epi_rms_bf16(acc, extra, outs):
    outs[0][...] = _rms(acc, extra[0][...]).astype(BF16)


def _row_positions(tm, seq, n_prompt, past_len):
    r = pl.program_id(0) * tm + lax.broadcasted_iota(jnp.int32, (tm, 1), 0)
    pos = jnp.where(r < n_prompt, r % seq, past_len)
    return pos.astype(F32)


def _rope_tables(pos, half):
    lane = lax.broadcasted_iota(jnp.int32, (1, V7X_LANES), 1)
    inv_freq = jnp.exp(-math.log(ROPE_BASE) * (lane % half).astype(F32) / half)
    ang = pos * inv_freq
    return jnp.cos(ang), jnp.sin(ang), lane


def _rope_lanes(x, pos, half):
    cos, sin, lane = _rope_tables(pos, half)
    first = lane < half
    second = (lane >= half) & (lane < 2 * half)
    x2_at_x1 = pltpu.roll(x, V7X_LANES - half, axis=1)
    x1_at_x2 = pltpu.roll(x, half, axis=1)
    out = jnp.where(first, x * cos - x2_at_x1 * sin, jnp.where(second, x * cos + x1_at_x2 * sin, 0.0))
    return out


def _ffn_kernel(h_ref, wg_ref, wu_ref, wo_ref, o_ref):
    j = pl.program_id(1)
    h = h_ref[...]
    g = jnp.dot(h, wg_ref[...], preferred_element_type=F32)
    u = jnp.dot(h, wu_ref[...], preferred_element_type=F32)
    act = (g * (1.0 / (1.0 + jnp.exp(-g))) * u).astype(BF16)
    part = jnp.dot(act, wo_ref[...], preferred_element_type=F32)

    @pl.when(j == 0)
    def _():
        o_ref[...] = part

    @pl.when(j != 0)
    def _():
        o_ref[...] += part


def ffn(h, w_in, w_out):
    m, d = h.shape
    f = w_out.shape[0]
    tf = 256 if f % 256 == 0 else f
    nf = f // tf
    tm = _tile(m, 640)
    return pl.pallas_call(
        _ffn_kernel, grid=(m // tm, nf),
        in_specs=[pl.BlockSpec((tm, d), lambda i, j: (i, 0)),
                  pl.BlockSpec((d, tf), lambda i, j: (0, j)),
                  pl.BlockSpec((d, tf), lambda i, j: (0, j + nf)),
                  pl.BlockSpec((tf, d), lambda i, j: (j, 0))],
        out_specs=pl.BlockSpec((tm, d), lambda i, j: (i, 0)),
        out_shape=jax.ShapeDtypeStruct((m, d), F32),
        compiler_params=_cparams(("parallel", "arbitrary"), 52), name="ffn",
    )(h, w_in, w_in, w_out)


class _Dims:
    def __init__(self, batch, seq, dec, past, page):
        self.batch, self.seq, self.dec, self.past, self.page = batch, seq, dec, past, page
        self.n_prompt = batch * seq
        self.n_pages = past // page
        self.qblocks = seq // QBLOCK
        assert seq % QBLOCK == 0 and past % page == 0 and page % NSA_BLOCK == 0


def _dot_nt(a, b):
    return lax.dot_general(a, b, (((1,), (1,)), ((), ())), preferred_element_type=F32)


def _dot(a, b):
    return jnp.dot(a, b, preferred_element_type=F32)


def _dot_f32_by_01(x, u):
    hi = x.astype(BF16)
    r1 = x - hi.astype(F32)
    mid = r1.astype(BF16)
    lo = (r1 - mid.astype(F32)).astype(BF16)
    return _dot(hi, u) + _dot(mid, u) + _dot(lo, u)


def _stack_heads(q_ref, qs_ref, n_heads, stride=HEAD_DIM, offset=0, width=HEAD_DIM):
    for h in range(n_heads):
        lo = offset + h * stride
        qs_ref[h * QBLOCK:(h + 1) * QBLOCK, :] = q_ref[:, lo:lo + width]


def _strict_upper(n):
    r = lax.broadcasted_iota(jnp.int32, (n, n), 0)
    c = lax.broadcasted_iota(jnp.int32, (n, n), 1)
    return jnp.where(r > c, 1.0, 0.0).astype(BF16)


def _log_sigmoid_pair(z):
    t = jnp.log1p(jnp.exp(-jnp.abs(z)))
    return jnp.minimum(z, 0.0) - t, -jnp.maximum(z, 0.0) - t


def _psb_kernel(q_ref, k_ref, v_ref, o_ref, qs_ref, acc_ref, carry_ref, *, hg, scale):
    qb = pl.program_id(1)
    rows = hg * QBLOCK
    _stack_heads(q_ref, qs_ref, hg)
    acc_ref[...] = jnp.zeros_like(acc_ref)
    carry_ref[...] = jnp.zeros_like(carry_ref)
    t_row = lax.broadcasted_iota(jnp.int32, (rows, QBLOCK), 0) % QBLOCK
    lane = lax.broadcasted_iota(jnp.int32, (rows, QBLOCK), 1)
    rel = lane - t_row
    upper = _strict_upper(QBLOCK)

    def body(i, carry):
        off = pl.multiple_of((qb - i) * QBLOCK, QBLOCK)
        kc = k_ref[pl.ds(off, QBLOCK), :]
        vc = v_ref[pl.ds(off, QBLOCK), :]
        z = _dot_nt(qs_ref[...], kc) * scale
        valid = rel < i * QBLOCK
        ls_pos, ls_neg = _log_sigmoid_pair(z)
        log_keep = jnp.where(valid, ls_neg, 0.0)
        later = _dot_f32_by_01(log_keep, upper) + carry_ref[...]
        a = jnp.where(valid, jnp.exp(ls_pos + later), 0.0)
        acc_ref[...] += _dot(a.astype(BF16), vc)
        carry_ref[...] += jnp.sum(log_keep, axis=-1, keepdims=True)
        return carry

    lax.fori_loop(0, qb + 1, body, 0)
    for h in range(hg):
        o_ref[:, h * HEAD_DIM:(h + 1) * HEAD_DIM] = acc_ref[h * QBLOCK:(h + 1) * QBLOCK, :].astype(o_ref.dtype)


def psb_attention(q, kvb, dims, hg=8):
    nq = N_HEADS * HEAD_DIM
    per_group = N_HEADS // SB_KV
    hg = min(hg, per_group)
    n_hg = N_HEADS // hg
    qbs, seq = dims.qblocks, dims.seq
    kern = functools.partial(_psb_kernel, hg=hg, scale=HEAD_DIM ** -0.5)
    return pl.pallas_call(
        kern, grid=(dims.batch, qbs, n_hg),
        in_specs=[pl.BlockSpec((QBLOCK, hg * HEAD_DIM), lambda b, i, g: (b * qbs + i, g)),
                  pl.BlockSpec((seq, HEAD_DIM), lambda b, i, g: (b, (g * hg) // per_group)),
                  pl.BlockSpec((seq, HEAD_DIM), lambda b, i, g: (b, SB_KV + (g * hg) // per_group))],
        out_specs=pl.BlockSpec((QBLOCK, hg * HEAD_DIM), lambda b, i, g: (b * qbs + i, g)),
        out_shape=jax.ShapeDtypeStruct((dims.n_prompt, nq), BF16),
        scratch_shapes=[pltpu.VMEM((hg * QBLOCK, HEAD_DIM), BF16),
                        pltpu.VMEM((hg * QBLOCK, HEAD_DIM), F32),
                        pltpu.VMEM((hg * QBLOCK, 1), F32)],
        compiler_params=_cparams(("parallel", "parallel", "arbitrary"), 32), name="sb_prompt_attention",
    )(q, kvb, kvb)


def _dsb_kernel(pt_ref, q_ref, *refs, pp, scale):
    page_refs = refs[:pp]
    o_ref, acc_ref, carry_ref = refs[pp:]
    step = pl.program_id(1)
    per_group = N_HEADS // SB_KV

    @pl.when(step == 0)
    def _():
        acc_ref[...] = jnp.zeros_like(acc_ref)
        carry_ref[...] = jnp.zeros_like(carry_ref)

    q = q_ref[0]
    page = page_refs[0].shape[1]
    upper = _strict_upper(page)
    head = lax.broadcasted_iota(jnp.int32, (N_HEADS, page), 0)
    first_group = head < per_group
    for i in range(pp):
        kv = page_refs[i][0]
        k_both = kv[:, :SB_KV * HEAD_DIM].astype(BF16)
        v_both = kv[:, SB_KV * HEAD_DIM:].astype(BF16)
        zz = jnp.concatenate([_dot_nt(q, k_both[:, g * HEAD_DIM:(g + 1) * HEAD_DIM]) for g in range(SB_KV)], axis=1)
        z = jnp.where(first_group, zz[:, :page], zz[:, page:]) * scale
        ls_pos, ls_neg = _log_sigmoid_pair(z)
        later = _dot_f32_by_01(ls_neg, upper) + carry_ref[...]
        a = jnp.exp(ls_pos + later).astype(BF16)
        oo = _dot(a, v_both)
        acc_ref[...] += jnp.where(first_group, oo[:, :HEAD_DIM], oo[:, HEAD_DIM:])
        carry_ref[...] += jnp.sum(ls_neg, axis=-1, keepdims=True)

    @pl.when(step == pl.num_programs(1) - 1)
    def _():
        o_ref[0] = acc_ref[...]


def dsb_attention(q_s, cache, page_table, dims, pp=8):
    assert SB_KV == 2 and HEAD_DIM == dims.page
    n_pages = dims.n_pages
    pp = min(pp, n_pages)
    assert n_pages % pp == 0
    width = cache.shape[-1]

    def page_spec(i):
        return pl.BlockSpec((1, dims.page, width), lambda s, g, pt: (pt[s, n_pages - 1 - (g * pp + i)], 0, 0))

    kern = functools.partial(_dsb_kernel, pp=pp, scale=HEAD_DIM ** -0.5)
    grid_spec = pltpu.PrefetchScalarGridSpec(
        num_scalar_prefetch=1, grid=(dims.dec, n_pages // pp),
        in_specs=[pl.BlockSpec((1, N_HEADS, HEAD_DIM), lambda s, g, pt: (s, 0, 0))] + [page_spec(i) for i in range(pp)],
        out_specs=pl.BlockSpec((1, N_HEADS, HEAD_DIM), lambda s, g, pt: (s, 0, 0)),
        scratch_shapes=[pltpu.VMEM((N_HEADS, HEAD_DIM), F32), pltpu.VMEM((N_HEADS, 1), F32)])
    return pl.pallas_call(
        kern, grid_spec=grid_spec, out_shape=jax.ShapeDtypeStruct((dims.dec, N_HEADS, HEAD_DIM), F32),
        compiler_params=_cparams(("parallel", "arbitrary"), 32), name="sb_decode_attention",
    )(page_table, q_s, *([cache] * pp))


def _sample_heads(x, dims, width):
    return x[dims.n_prompt:].reshape(dims.dec, -1, width)


def _join(o_prompt, o_sample, dims):
    return jnp.concatenate([o_prompt, o_sample.reshape(dims.dec, -1).astype(o_prompt.dtype)], axis=0)


def sb_mixer(h, x_res, dims, cache, page_table, w_in, w_o):
    nq = N_HEADS * HEAD_DIM
    q, = dense(h, w_in[:, :nq].astype(BF16), _epi_bf16, [(BF16, nq, min(512, nq))], name="sb_q")
    nkv = w_in.shape[1] - nq
    kv, kvb = dense(h, w_in[:, nq:].astype(BF16), _epi_f32_bf16, [(F32, nkv, nkv), (BF16, nkv, nkv)], tn=nkv, name="sb_kv")
    o_p = psb_attention(q, kvb, dims)
    o_s = dsb_attention(_sample_heads(q, dims, HEAD_DIM), cache.reshape(cache.shape[0], dims.page, -1), page_table, dims)
    o = _join(o_p, o_s, dims)
    y, = dense(o, w_o.astype(BF16), _epi_residual, [(F32, w_o.shape[1], min(512, w_o.shape[1]))],
               extras=[(x_res, "tile")], name="sb_out")
    return y, kv


def _rel_bucket(dist):
    max_exact = REL_BUCKETS // 2
    d = jnp.maximum(dist, 0)
    log_ratio = jnp.log(jnp.maximum(d, 1).astype(F32) / max_exact) / math.log(REL_MAX_DIST / max_exact)
    far = max_exact + (log_ratio * (REL_BUCKETS - max_exact)).astype(jnp.int32)
    return jnp.where(d < max_exact, d, jnp.minimum(far, REL_BUCKETS - 1))


def _bias_rows(dist, tab_t):
    bucket = _rel_bucket(dist)
    out = jnp.zeros((tab_t.shape[0], dist.shape[1]), F32)
    for b in range(REL_BUCKETS):
        out = jnp.where(bucket == b, tab_t[:, b:b + 1], out)
    return out


def _bias_tiles_kernel(tab_ref, o_ref):
    c = pl.program_id(0)
    i = lax.broadcasted_iota(jnp.int32, (QBLOCK, QBLOCK), 0)
    j = lax.broadcasted_iota(jnp.int32, (QBLOCK, QBLOCK), 1)
    for cls in range(3):
        bucket = _rel_bucket(i - j + cls * QBLOCK)
        val = jnp.zeros((QBLOCK, QBLOCK), F32)
        for b in range(REL_BUCKETS):
            val = jnp.where(bucket == b, tab_ref[b, c], val)
        o_ref[cls] = val


def bias_tiles(rel_table):
    assert REL_MAX_DIST <= QBLOCK + 1
    n_cols = rel_table.shape[1]
    return pl.pallas_call(
        _bias_tiles_kernel, grid=(n_cols,),
        in_specs=[pl.BlockSpec(memory_space=pltpu.SMEM)],
        out_specs=pl.BlockSpec((3, QBLOCK, QBLOCK), lambda c: (0, c, 0)),
        out_shape=jax.ShapeDtypeStruct((3, n_cols * QBLOCK, QBLOCK), F32),
        compiler_params=_cparams(("parallel",), 16), name="rel_bias_tiles",
    )(rel_table)


def _online_softmax_step(s, vc, m_ref, l_ref, acc_ref, mask=None):
    m_old = m_ref[...]
    m_new = jnp.maximum(m_old, jnp.max(s, axis=-1, keepdims=True))
    alpha = jnp.exp(m_old - m_new)
    p = jnp.exp(s - m_new)
    if mask is not None:
        p = jnp.where(mask, p, 0.0)
    l_ref[...] = alpha * l_ref[...] + jnp.sum(p, axis=-1, keepdims=True)
    acc_ref[...] = alpha * acc_ref[...] + _dot(p.astype(BF16), vc)
    m_ref[...] = m_new


def _softmax_init(m_ref, l_ref, acc_ref):
    m_ref[...] = jnp.full(m_ref.shape, NEG_BIG, F32)
    l_ref[...] = jnp.zeros_like(l_ref)
    acc_ref[...] = jnp.zeros_like(acc_ref)


def _causal_rel(rows):
    t_row = lax.broadcasted_iota(jnp.int32, (rows, QBLOCK), 0) % QBLOCK
    lane = lax.broadcasted_iota(jnp.int32, (rows, QBLOCK), 1)
    return lane - t_row


def _diff_lambda(lam_ref):
    lp = lam_ref[...]
    return (jnp.exp(jnp.sum(lp[0:1] * lp[1:2], axis=-1, keepdims=True))
            - jnp.exp(jnp.sum(lp[2:3] * lp[3:4], axis=-1, keepdims=True)) + DIFF_LAMBDA_INIT)


def _diff_finish(o0, o1, lam, g_sub):
    o = o0 - lam * o1
    return _rms(o, g_sub) * (1.0 - DIFF_LAMBDA_INIT)


def _pdiff_kernel(q_ref, k_ref, v_ref, bias_ref, lam_ref, g_ref, o_ref, qs_ref, m_ref, l_ref, acc_ref, *, hg, scale):
    qb = pl.program_id(1)
    rows = hg * QBLOCK
    d = DIFF_D
    for i in range(2):
        _stack_heads(q_ref, qs_ref.at[i], hg, stride=2 * d, offset=i * d)
        _softmax_init(m_ref.at[i], l_ref.at[i], acc_ref.at[i])
    rel = _causal_rel(rows)

    def body(c, carry):
        off = pl.multiple_of(c * QBLOCK, QBLOCK)
        kc = k_ref[pl.ds(off, QBLOCK), :]
        vc = v_ref[pl.ds(off, QBLOCK), :]
        cls = jnp.minimum(qb - c, 2)
        valid = rel <= (qb - c) * QBLOCK
        for i in range(2):
            s = _dot_nt(qs_ref[i], kc[:, i * d:(i + 1) * d]) * scale + bias_ref[cls, i]
            s = jnp.where(valid, s, NEG_BIG)
            _online_softmax_step(s, vc, m_ref.at[i], l_ref.at[i], acc_ref.at[i])
        return carry

    lax.fori_loop(0, qb + 1, body, 0)
    res = _diff_finish(acc_ref[0] / l_ref[0], acc_ref[1] / l_ref[1], _diff_lambda(lam_ref), g_ref[...])
    for h in range(hg):
        o_ref[:, h * 2 * d:(h + 1) * 2 * d] = res[h * QBLOCK:(h + 1) * QBLOCK, :].astype(o_ref.dtype)


def pdiff_attention(q, kvb, bias, lam_p, g_sub, dims, hg=4):
    d = DIFF_D
    hg = min(hg, DIFF_HEADS)
    qbs, seq = dims.qblocks, dims.seq
    rows = hg * QBLOCK
    kern = functools.partial(_pdiff_kernel, hg=hg, scale=d ** -0.5)
    return pl.pallas_call(
        kern, grid=(dims.batch, qbs, DIFF_HEADS // hg),
        in_specs=[pl.BlockSpec((QBLOCK, hg * 2 * d), lambda b, i, g: (b * qbs + i, g)),
                  pl.BlockSpec((seq, 2 * d), lambda b, i, g: (b, 0)),
                  pl.BlockSpec((seq, 2 * d), lambda b, i, g: (b, 1)),
                  pl.BlockSpec((3, 2, rows, QBLOCK), lambda b, i, g: (0, 0, g, 0)),
                  pl.BlockSpec((4, d), lambda b, i, g: (0, 0)),
                  pl.BlockSpec((1, 2 * d), lambda b, i, g: (0, 0))],
        out_specs=pl.BlockSpec((QBLOCK, hg * 2 * d), lambda b, i, g: (b * qbs + i, g)),
        out_shape=jax.ShapeDtypeStruct((dims.n_prompt, DIFF_HEADS * 2 * d), BF16),
        scratch_shapes=[pltpu.VMEM((2, rows, d), BF16), pltpu.VMEM((2, rows, 1), F32),
                        pltpu.VMEM((2, rows, 1), F32), pltpu.VMEM((2, rows, 2 * d), F32)],
        compiler_params=_cparams(("parallel", "parallel", "arbitrary"), 32), name="diff_prompt_attention",
    )(q, kvb, kvb, bias, lam_p, g_sub.reshape(1, -1))


def _merge_new_row(s_new, v_new, m_ref, l_ref, acc_ref):
    m_old = m_ref[...]
    m_new = jnp.maximum(m_old, s_new)
    alpha = jnp.exp(m_old - m_new)
    p = jnp.exp(s_new - m_new)
    l_ref[...] = alpha * l_ref[...] + p
    acc_ref[...] = alpha * acc_ref[...] + p * v_new
    m_ref[...] = m_new


def _ddiff_kernel(pt_ref, q_ref, new_ref, tab_ref, lam_ref, g_ref, *refs, pp, scale, past):
    page_refs = refs[:pp]
    o_ref, m_ref, l_ref, acc_ref = refs[pp:]
    step = pl.program_id(1)
    d = DIFF_D
    n_rows = 2 * DIFF_HEADS

    @pl.when(step == 0)
    def _():
        _softmax_init(m_ref, l_ref, acc_ref)

    q = q_ref[0]
    page = page_refs[0].shape[1]
    first_map = lax.broadcasted_iota(jnp.int32, (n_rows, page), 0) < DIFF_HEADS
    lane = lax.broadcasted_iota(jnp.int32, (1, page), 1)
    tab_t = tab_ref[...]
    for i in range(pp):
        kv = page_refs[i][0]
        kb = kv[:, :2 * d].astype(BF16)
        vb = kv[:, 2 * d:].astype(BF16)
        ss = jnp.concatenate([_dot_nt(q, kb[:, :d]), _dot_nt(q, kb[:, d:])], axis=1)
        dist = past - ((step * pp + i) * page + lane)
        s = jnp.where(first_map, ss[:, :page], ss[:, page:]) * scale + _bias_rows(dist, tab_t)
        _online_softmax_step(s, vb, m_ref, l_ref, acc_ref)

    @pl.when(step == pl.num_programs(1) - 1)
    def _():
        new = new_ref[0].astype(F32)
        qf = q.astype(F32)
        first = lax.broadcasted_iota(jnp.int32, (n_rows, 1), 0) < DIFF_HEADS
        s1 = jnp.sum(qf * new[:, :d], axis=-1, keepdims=True)
        s2 = jnp.sum(qf * new[:, d:2 * d], axis=-1, keepdims=True)
        s_new = jnp.where(first, s1, s2) * scale + tab_t[:, 0:1]
        _merge_new_row(s_new, new[:, 2 * d:], m_ref, l_ref, acc_ref)
        on = acc_ref[...] / l_ref[...]
        o_ref[0] = _diff_finish(on[:DIFF_HEADS], on[DIFF_HEADS:], _diff_lambda(lam_ref), g_ref[...])


def ddiff_attention(q_s, new_rows, cache, page_table, tab_t, lam_p, g_sub, dims, pp=8):
    d = DIFF_D
    n_pages = dims.n_pages
    pp = min(pp, n_pages)
    assert n_pages % pp == 0
    n_rows = 2 * DIFF_HEADS

    def page_spec(i):
        return pl.BlockSpec((1, dims.page, 4 * d), lambda s, g, pt: (pt[s, g * pp + i], 0, 0))

    def whole(shape):
        return pl.BlockSpec(shape, lambda s, g, pt: (0,) * len(shape))

    kern = functools.partial(_ddiff_kernel, pp=pp, scale=d ** -0.5, past=dims.past)
    grid_spec = pltpu.PrefetchScalarGridSpec(
        num_scalar_prefetch=1, grid=(dims.dec, n_pages // pp),
        in_specs=[pl.BlockSpec((1, n_rows, d), lambda s, g, pt: (s, 0, 0)),
                  pl.BlockSpec((1, 1, 4 * d), lambda s, g, pt: (s, 0, 0)),
                  whole((n_rows, REL_BUCKETS)), whole((4, d)), whole((1, 2 * d))]
                 + [page_spec(i) for i in range(pp)],
        out_specs=pl.BlockSpec((1, DIFF_HEADS, 2 * d), lambda s, g, pt: (s, 0, 0)),
        scratch_shapes=[pltpu.VMEM((n_rows, 1), F32), pltpu.VMEM((n_rows, 1), F32), pltpu.VMEM((n_rows, 2 * d), F32)])
    return pl.pallas_call(
        kern, grid_spec=grid_spec, out_shape=jax.ShapeDtypeStruct((dims.dec, DIFF_HEADS, 2 * d), F32),
        compiler_params=_cparams(("parallel", "arbitrary"), 32), name="diff_decode_attention",
    )(page_table, q_s, new_rows, tab_t, lam_p, g_sub.reshape(1, -1), *([cache] * pp))


def diff_mixer(h, x_res, dims, cache, page_table, rel_table, tiles, w_in, lam_p, g_sub, w_o):
    d = DIFF_D
    nq = DIFF_HEADS * 2 * d
    q, = dense(h, w_in[:, :nq].astype(BF16), _epi_bf16, [(BF16, nq, min(512, nq))], name="diff_q")
    nkv = w_in.shape[1] - nq
    kv, kvb = dense(h, w_in[:, nq:].astype(BF16), _epi_f32_bf16, [(F32, nkv, nkv), (BF16, nkv, nkv)], tn=nkv, name="diff_kv")
    bias = tiles.reshape(3, DIFF_HEADS, 2, QBLOCK, QBLOCK).transpose(0, 2, 1, 3, 4).reshape(3, 2, DIFF_HEADS * QBLOCK, QBLOCK)
    o_p = pdiff_attention(q, kvb, bias, lam_p, g_sub, dims)
    q_s = q[dims.n_prompt:].reshape(dims.dec, DIFF_HEADS, 2, d).transpose(0, 2, 1, 3).reshape(dims.dec, 2 * DIFF_HEADS, d)
    tab_t = rel_table.T.reshape(DIFF_HEADS, 2, REL_BUCKETS).transpose(1, 0, 2).reshape(2 * DIFF_HEADS, REL_BUCKETS)
    o_s = ddiff_attention(q_s, kvb[dims.n_prompt:].reshape(dims.dec, 1, nkv), cache.reshape(cache.shape[0], dims.page, -1),
                          page_table, tab_t, lam_p, g_sub, dims)
    o = _join(o_p, o_s, dims)
    y, = dense(o, w_o.astype(BF16), _epi_residual, [(F32, w_o.shape[1], min(512, w_o.shape[1]))],
               extras=[(x_res, "tile")], name="diff_out")
    return y, kv


MLA_PAD = 128


def _mla_kv_epilogue(acc, extra, outs, *, rank, seq, n_prompt, past):
    tm = acc.shape[0]
    pos = _row_positions(tm, seq, n_prompt, past)
    ckv = _rms(acc[:, :rank], extra[0][...])
    pe = _rope_lanes(acc[:, rank:rank + MLA_PAD], pos, MLA_ROPE // 2)
    outs[0][:, :rank] = ckv
    outs[0][:, rank:rank + MLA_ROPE] = pe[:, :MLA_ROPE]
    outs[1][:, :rank] = ckv.astype(BF16)
    outs[1][:, rank:] = pe.astype(BF16)


def _mla_q_kernel(cq_ref, wn_ref, wp_ref, wk_ref, o_ref, *, rank, seq, n_prompt, past):
    tm = cq_ref.shape[0]
    cq = cq_ref[...]
    q_nope = _dot(cq, wn_ref[...]).astype(BF16)
    q_lat = _dot(q_nope, wk_ref[0])
    pos = _row_positions(tm, seq, n_prompt, past)
    q_pe = _rope_lanes(_dot(cq, wp_ref[...]), pos, MLA_ROPE // 2)
    o_ref[0, :, :rank] = q_lat.astype(BF16)
    o_ref[0, :, rank:] = q_pe.astype(BF16)


def mla_queries(c_q, w_nope, w_pe, wk_t, dims):
    m, qr = c_q.shape
    rank = wk_t.shape[2]
    tm = _tile(m, 640)
    kern = functools.partial(_mla_q_kernel, rank=rank, seq=dims.seq, n_prompt=dims.n_prompt, past=dims.past)
    return pl.pallas_call(
        kern, grid=(m // tm, N_HEADS),
        in_specs=[pl.BlockSpec((tm, qr), lambda i, h: (i, 0)),
                  pl.BlockSpec((qr, MLA_NOPE), lambda i, h: (0, h)),
                  pl.BlockSpec((qr, MLA_PAD), lambda i, h: (0, h)),
                  pl.BlockSpec((1, MLA_NOPE, rank), lambda i, h: (h, 0, 0))],
        out_specs=pl.BlockSpec((1, tm, rank + MLA_PAD), lambda i, h: (h, i, 0)),
        out_shape=jax.ShapeDtypeStruct((N_HEADS, m, rank + MLA_PAD), BF16),
        compiler_params=_cparams(("parallel", "arbitrary"), 32), name="mla_queries",
    )(c_q, w_nope, w_pe, wk_t)


def _pmla_kernel(q_ref, k_ref, wv_ref, o_ref, m_ref, l_ref, acc_ref, *, hg, rank, scale):
    qb = pl.program_id(1)
    rows = hg * QBLOCK
    _softmax_init(m_ref, l_ref, acc_ref)
    rel = _causal_rel(rows)
    qs = q_ref[...].reshape(rows, q_ref.shape[2])

    def body(c, carry):
        off = pl.multiple_of(c * QBLOCK, QBLOCK)
        kc = k_ref[pl.ds(off, QBLOCK), :]
        s = _dot_nt(qs, kc) * scale
        s = jnp.where(rel <= (qb - c) * QBLOCK, s, NEG_BIG)
        _online_softmax_step(s, kc[:, :rank], m_ref, l_ref, acc_ref)
        return carry

    lax.fori_loop(0, qb + 1, body, 0)
    o_lat = (acc_ref[...] / l_ref[...]).astype(BF16)
    for h in range(hg):
        o_ref[:, h * MLA_V:(h + 1) * MLA_V] = _dot(o_lat[h * QBLOCK:(h + 1) * QBLOCK], wv_ref[h]).astype(o_ref.dtype)


def pmla_attention(q_cat, kb, wv, dims, hg=8):
    width = kb.shape[1]
    rank = wv.shape[1]
    hg = min(hg, N_HEADS)
    qbs, seq = dims.qblocks, dims.seq
    rows = hg * QBLOCK
    kern = functools.partial(_pmla_kernel, hg=hg, rank=rank, scale=(MLA_NOPE + MLA_ROPE) ** -0.5)
    return pl.pallas_call(
        kern, grid=(dims.batch, qbs, N_HEADS // hg),
        in_specs=[pl.BlockSpec((hg, QBLOCK, width), lambda b, i, g: (g, b * qbs + i, 0)),
                  pl.BlockSpec((seq, width), lambda b, i, g: (b, 0)),
                  pl.BlockSpec((hg, rank, MLA_V), lambda b, i, g: (g, 0, 0))],
        out_specs=pl.BlockSpec((QBLOCK, hg * MLA_V), lambda b, i, g: (b * qbs + i, g)),
        out_shape=jax.ShapeDtypeStruct((dims.n_prompt, N_HEADS * MLA_V), BF16),
        scratch_shapes=[pltpu.VMEM((rows, 1), F32), pltpu.VMEM((rows, 1), F32), pltpu.VMEM((rows, rank), F32)],
        compiler_params=_cparams(("parallel", "parallel", "arbitrary"), 40), name="mla_prompt_attention",
    )(q_cat, kb, wv)


def _dmla_kernel(pt_ref, q_ref, new_ref, *refs, pp, rank, scale):
    page_refs = refs[:pp]
    o_ref, m_ref, l_ref, acc_ref = refs[pp:]
    step = pl.program_id(1)

    @pl.when(step == 0)
    def _():
        _softmax_init(m_ref, l_ref, acc_ref)

    q = q_ref[0]
    for i in range(pp):
        kb = page_refs[i][0].astype(BF16)
        s = (_dot_nt(q[:, :rank], kb[:, :rank]) + _dot_nt(q[:, rank:rank + MLA_ROPE], kb[:, rank:])) * scale
        _online_softmax_step(s, kb[:, :rank], m_ref, l_ref, acc_ref)

    @pl.when(step == pl.num_programs(1) - 1)
    def _():
        new = new_ref[0].astype(F32)
        s_new = jnp.sum(q.astype(F32) * new, axis=-1, keepdims=True) * scale
        _merge_new_row(s_new, new[:, :rank], m_ref, l_ref, acc_ref)
        o_ref[0] = (acc_ref[...] / l_ref[...]).astype(o_ref.dtype)


def dmla_attention(q_s, new_rows, cache, page_table, dims, pp=8):
    width = q_s.shape[2]
    rank = width - MLA_PAD
    n_pages = dims.n_pages
    pp = min(pp, n_pages)
    assert n_pages % pp == 0

    def page_spec(i):
        return pl.BlockSpec((1, dims.page, cache.shape[2]), lambda s, g, pt: (pt[s, g * pp + i], 0, 0))

    kern = functools.partial(_dmla_kernel, pp=pp, rank=rank, scale=(MLA_NOPE + MLA_ROPE) ** -0.5)
    grid_spec = pltpu.PrefetchScalarGridSpec(
        num_scalar_prefetch=1, grid=(dims.dec, n_pages // pp),
        in_specs=[pl.BlockSpec((1, N_HEADS, width), lambda s, g, pt: (s, 0, 0)),
                  pl.BlockSpec((1, 1, width), lambda s, g, pt: (s, 0, 0))] + [page_spec(i) for i in range(pp)],
        out_specs=pl.BlockSpec((1, N_HEADS, rank), lambda s, g, pt: (s, 0, 0)),
        scratch_shapes=[pltpu.VMEM((N_HEADS, 1), F32), pltpu.VMEM((N_HEADS, 1), F32), pltpu.VMEM((N_HEADS, rank), F32)])
    return pl.pallas_call(
        kern, grid_spec=grid_spec, out_shape=jax.ShapeDtypeStruct((dims.dec, N_HEADS, rank), BF16),
        compiler_params=_cparams(("parallel", "arbitrary"), 32), name="mla_decode_attention",
    )(page_table, q_s, new_rows, *([cache] * pp))


def _mla_up_kernel(o_ref, wv_ref, out_ref):
    out_ref[...] = _dot(o_ref[0], wv_ref[0]).astype(out_ref.dtype)


def mla_up(o_lat_t, wv):
    n_h, s, rank = o_lat_t.shape
    return pl.pallas_call(
        _mla_up_kernel, grid=(n_h,),
        in_specs=[pl.BlockSpec((1, s, rank), lambda h: (h, 0, 0)), pl.BlockSpec((1, rank, MLA_V), lambda h: (h, 0, 0))],
        out_specs=pl.BlockSpec((s, MLA_V), lambda h: (0, h)),
        out_shape=jax.ShapeDtypeStruct((s, n_h * MLA_V), BF16),
        compiler_params=_cparams(("parallel",), 16), name="mla_value_up",
    )(o_lat_t, wv)


def mla_mixer(h, x_res, dims, cache, page_table, w_in, g_q, g_kv, w_uq, w_ukv, w_o):
    rank = w_ukv.shape[0]
    q_rank = w_uq.shape[0]
    half = MLA_ROPE // 2
    c_q, = dense(h, w_in[:, :q_rank].astype(BF16), _epi_rms_bf16, [(BF16, q_rank, q_rank)],
                 extras=[(g_q.reshape(1, -1), "full")], tn=q_rank, name="mla_cq")
    w_kv = jnp.pad(w_in[:, q_rank:], ((0, 0), (0, MLA_PAD - MLA_ROPE))).astype(BF16)
    epi = functools.partial(_mla_kv_epilogue, rank=rank, seq=dims.seq, n_prompt=dims.n_prompt, past=dims.past)
    rows, kb = dense(h, w_kv, epi, [(F32, rank + MLA_ROPE, rank + MLA_ROPE), (BF16, rank + MLA_PAD, rank + MLA_PAD)],
                     extras=[(g_kv.reshape(1, -1), "full")], tn=rank + MLA_PAD, name="mla_kv")
    w_uq3 = w_uq.reshape(q_rank, N_HEADS, MLA_NOPE + MLA_ROPE)
    w_nope = w_uq3[:, :, :MLA_NOPE].reshape(q_rank, -1).astype(BF16)
    w_pe = jnp.pad(w_uq3[:, :, MLA_NOPE:], ((0, 0), (0, 0), (0, MLA_PAD - MLA_ROPE))).reshape(q_rank, -1).astype(BF16)
    wk_t = jnp.transpose(w_ukv[:, :, :MLA_NOPE], (1, 2, 0)).astype(BF16)
    wv = jnp.transpose(w_ukv[:, :, MLA_NOPE:], (1, 0, 2)).astype(BF16)
    q_cat = mla_queries(c_q, w_nope, w_pe, wk_t, dims)
    o_p = pmla_attention(q_cat, kb, wv, dims)
    q_s = jnp.transpose(q_cat[:, dims.n_prompt:], (1, 0, 2))
    o_lat = dmla_attention(q_s, kb[dims.n_prompt:].reshape(dims.dec, 1, -1), cache, page_table, dims)
    o_s = mla_up(jnp.transpose(o_lat, (1, 0, 2)), wv)
    o = _join(o_p, o_s, dims)
    y, = dense(o, w_o.astype(BF16), _epi_residual, [(F32, w_o.shape[1], min(512, w_o.shape[1]))],
               extras=[(x_res, "tile")], name="mla_out")
    return y, rows


def _block_means(x, n_blocks):
    return jnp.sum(x.reshape(n_blocks, NSA_BLOCK, x.shape[1]), axis=1) * (1.0 / NSA_BLOCK)


def _rank_select(imp, blk, n_cols, topk):
    rank = jnp.zeros(imp.shape, jnp.int32)
    for n in range(n_cols):
        col = imp[:, n:n + 1]
        beats = (col > imp) | ((col == imp) & (blk > n))
        rank = rank + jnp.where(beats, 1, 0)
    return rank


def _pnsa_kernel(q_ref, gate_ref, kvf_ref, ksel_ref, vsel_ref, kwin_ref, vwin_ref, tiles_ref, tab_ref, o_ref,
                 qs_ref, kc_ref, vc_ref, imp_ref, oc_ref, selk_ref, m_s, l_s, acc_s, m_w, l_w, acc_w,
                 *, hg, n_groups, n_blocks, topk, scale):
    qb = pl.program_id(1)
    step = pl.program_id(2)
    rows = hg * QBLOCK
    hd = HEAD_DIM
    seq = kvf_ref.shape[0]
    t_abs = qb * QBLOCK + lax.broadcasted_iota(jnp.int32, (QBLOCK, 1), 0)
    _stack_heads(q_ref, qs_ref, hg)

    width = kc_ref.shape[0]

    @pl.when(step == 0)
    def _():
        means = _block_means(kvf_ref[...], n_blocks)
        if width > n_blocks:
            kc_ref[...] = jnp.zeros_like(kc_ref)
            vc_ref[...] = jnp.zeros_like(vc_ref)
        kc_ref[0:n_blocks, :] = means[:, :hd].astype(BF16)
        vc_ref[0:n_blocks, :] = means[:, hd:].astype(BF16)
        imp_ref[...] = jnp.zeros_like(imp_ref)

    @pl.when(step < n_groups)
    def _():
        blk = lax.broadcasted_iota(jnp.int32, (QBLOCK, width), 1)
        dist = t_abs - (blk * NSA_BLOCK + NSA_BLOCK - 1)
        bucket = _rel_bucket(dist)
        per_head = []
        for h in range(hg):
            bias_h = jnp.zeros((QBLOCK, width), F32)
            for b in range(REL_BUCKETS):
                bias_h = jnp.where(bucket == b, tab_ref[b, step * hg + h], bias_h)
            per_head.append(bias_h)
        s = _dot_nt(qs_ref[...], kc_ref[...]) * scale + jnp.concatenate(per_head, axis=0)
        s = s.reshape(hg, QBLOCK, width)
        mask = (dist >= 0)[None]
        s = jnp.where(mask, s, NEG_BIG)
        e = jnp.where(mask, jnp.exp(s - jnp.max(s, axis=-1, keepdims=True)), 0.0)
        p = e / jnp.maximum(jnp.sum(e, axis=-1, keepdims=True), 1e-30)
        imp_ref[...] += jnp.sum(p, axis=0)
        oc = _dot(p.reshape(rows, width).astype(BF16), vc_ref[...])
        oc_ref[pl.ds(pl.multiple_of(step * rows, rows), rows), :] = oc

    @pl.when(step == n_groups - 1)
    def _():
        blk = lax.broadcasted_iota(jnp.int32, (QBLOCK, width), 1)
        cur = t_abs // NSA_BLOCK
        forced = (blk == 0) | (blk == cur) | (blk == cur - 1)
        imp = jnp.where(forced, 2.0, imp_ref[...])
        imp = jnp.where(blk <= cur, imp, -1.0)
        rank = _rank_select(imp, blk, n_blocks, topk)
        sel = jnp.where((rank < topk) & (imp >= 0.0), 1.0, 0.0).astype(BF16)
        kb = lax.broadcasted_iota(jnp.int32, (width, seq), 1) // NSA_BLOCK
        nb = lax.broadcasted_iota(jnp.int32, (width, seq), 0)
        expand = jnp.where(kb == nb, 1.0, 0.0).astype(BF16)
        selk_ref[...] = _dot(sel, expand)

    @pl.when(step >= n_groups)
    def _():
        g = step - n_groups
        row0 = pl.multiple_of(g * rows, rows)
        rel = _causal_rel(QBLOCK)
        _softmax_init(m_s, l_s, acc_s)
        _softmax_init(m_w, l_w, acc_w)
        qs = qs_ref[...]

        def scores(kc, c):
            cls = jnp.minimum(qb - c, 2)
            s = _dot_nt(qs, kc) * scale + tiles_ref[cls, pl.ds(row0, rows), :]
            return s.reshape(hg, QBLOCK, QBLOCK)

        def sel_body(c, carry):
            off = pl.multiple_of(c * QBLOCK, QBLOCK)
            picked = selk_ref[:, pl.ds(off, QBLOCK)] > 0.5
            mask = (picked & (rel <= (qb - c) * QBLOCK))[None]
            s = jnp.where(mask, scores(ksel_ref[pl.ds(off, QBLOCK), :], c), NEG_BIG).reshape(rows, QBLOCK)
            _online_softmax_step(s, vsel_ref[pl.ds(off, QBLOCK), :], m_s, l_s, acc_s, mask=s > 0.5 * NEG_BIG)
            return carry

        lax.fori_loop(0, qb + 1, sel_body, 0)

        def win_body(c, carry):
            off = pl.multiple_of(c * QBLOCK, QBLOCK)
            dist = (qb - c) * QBLOCK - rel
            mask = ((dist >= 0) & (dist < NSA_WINDOW))[None]
            s = jnp.where(mask, scores(kwin_ref[pl.ds(off, QBLOCK), :], c), NEG_BIG).reshape(rows, QBLOCK)
            _online_softmax_step(s, vwin_ref[pl.ds(off, QBLOCK), :], m_w, l_w, acc_w, mask=s > 0.5 * NEG_BIG)
            return carry

        lax.fori_loop(jnp.maximum(qb - NSA_WINDOW // QBLOCK, 0), qb + 1, win_body, 0)
        o_c = oc_ref[pl.ds(row0, rows), :]
        o_s = acc_s[...] / l_s[...]
        o_w = acc_w[...] / l_w[...]
        gate = gate_ref[0]
        for h in range(hg):
            r = slice(h * QBLOCK, (h + 1) * QBLOCK)
            o = (gate[:, h:h + 1] * o_c[r] + gate[:, hg + h:hg + h + 1] * o_s[r]
                 + gate[:, 2 * hg + h:2 * hg + h + 1] * o_w[r])
            o_ref[:, h * hd:(h + 1) * hd] = o.astype(o_ref.dtype)


def pnsa_attention(q, gates_g, kv, kvb, tiles, rel_table, dims, hg=8):
    hd = HEAD_DIM
    hg = min(hg, N_HEADS)
    ng = N_HEADS // hg
    qbs, seq = dims.qblocks, dims.seq
    nb = seq // NSA_BLOCK
    nbw = -(-nb // V7X_LANES) * V7X_LANES
    rows = hg * QBLOCK
    kern = functools.partial(_pnsa_kernel, hg=hg, n_groups=ng, n_blocks=nb, topk=min(NSA_TOPK, nb), scale=hd ** -0.5)

    def chan(c):
        return pl.BlockSpec((seq, hd), lambda b, i, s: (b, c))

    return pl.pallas_call(
        kern, grid=(dims.batch, qbs, 2 * ng),
        in_specs=[pl.BlockSpec((QBLOCK, rows), lambda b, i, s: (b * qbs + i, s % ng)),
                  pl.BlockSpec((1, QBLOCK, 3 * hg), lambda b, i, s: (s % ng, b * qbs + i, 0)),
                  pl.BlockSpec((seq, 2 * hd), lambda b, i, s: (b, 0)),
                  chan(2), chan(3), chan(4), chan(5),
                  pl.BlockSpec(tiles.shape, lambda b, i, s: (0, 0, 0)),
                  pl.BlockSpec(memory_space=pltpu.SMEM)],
        out_specs=pl.BlockSpec((QBLOCK, rows), lambda b, i, s: (b * qbs + i, jnp.maximum(s - ng, 0))),
        out_shape=jax.ShapeDtypeStruct((dims.n_prompt, N_HEADS * hd), BF16),
        scratch_shapes=[pltpu.VMEM((rows, hd), BF16), pltpu.VMEM((nbw, hd), BF16), pltpu.VMEM((nbw, hd), BF16),
                        pltpu.VMEM((QBLOCK, nbw), F32), pltpu.VMEM((N_HEADS * QBLOCK, hd), F32),
                        pltpu.VMEM((QBLOCK, seq), F32),
                        pltpu.VMEM((rows, 1), F32), pltpu.VMEM((rows, 1), F32), pltpu.VMEM((rows, hd), F32),
                        pltpu.VMEM((rows, 1), F32), pltpu.VMEM((rows, 1), F32), pltpu.VMEM((rows, hd), F32)],
        compiler_params=_cparams(("parallel", "parallel", "arbitrary"), 48), name="nsa_prompt_attention",
    )(q, gates_g, kv, kvb, kvb, kvb, kvb, tiles, rel_table)


def _dnsa_compress_kernel(pt_ref, q_ref, tab_ref, *refs, pp, scale, past):
    page_refs = refs[:pp]
    oc_ref, imp_ref, kc_ref, vc_ref = refs[pp:]
    step = pl.program_id(1)
    hd = HEAD_DIM
    page = page_refs[0].shape[1]
    per_page = page // NSA_BLOCK
    n_new = pp * per_page
    means = _block_means(jnp.concatenate([page_refs[i][0] for i in range(pp)], axis=0), n_new)
    row0 = pl.multiple_of(step * n_new, n_new)
    kc_ref[pl.ds(row0, n_new), :] = means[:, :hd]
    vc_ref[pl.ds(row0, n_new), :] = means[:, hd:]

    @pl.when(step == pl.num_programs(1) - 1)
    def _():
        n_blocks = kc_ref.shape[0]
        blk = lax.broadcasted_iota(jnp.int32, (1, n_blocks), 1)
        dist = past - (blk * NSA_BLOCK + NSA_BLOCK - 1)
        s = _dot_nt(q_ref[0], kc_ref[...].astype(BF16)) * scale + _bias_rows(dist, tab_ref[...])
        mask = dist >= 0
        s = jnp.where(mask, s, NEG_BIG)
        e = jnp.where(mask, jnp.exp(s - jnp.max(s, axis=-1, keepdims=True)), 0.0)
        p = e / jnp.maximum(jnp.sum(e, axis=-1, keepdims=True), 1e-30)
        imp_ref[0] = jnp.sum(p, axis=0, keepdims=True)
        oc_ref[0] = _dot(p.astype(BF16), vc_ref[...].astype(BF16))


def dnsa_compress(q_s, cache, page_table, tab_t, dims, pp=8):
    hd = HEAD_DIM
    n_pages = dims.n_pages
    pp = min(pp, n_pages)
    assert n_pages % pp == 0 and (pp * dims.page // NSA_BLOCK) % 8 == 0
    nb = dims.past // NSA_BLOCK

    def page_spec(i):
        return pl.BlockSpec((1, dims.page, 2 * hd), lambda s, g, pt: (pt[s, g * pp + i], 0, 0))

    kern = functools.partial(_dnsa_compress_kernel, pp=pp, scale=hd ** -0.5, past=dims.past)
    grid_spec = pltpu.PrefetchScalarGridSpec(
        num_scalar_prefetch=1, grid=(dims.dec, n_pages // pp),
        in_specs=[pl.BlockSpec((1, N_HEADS, hd), lambda s, g, pt: (s, 0, 0)),
                  pl.BlockSpec(tab_t.shape, lambda s, g, pt: (0, 0))] + [page_spec(i) for i in range(pp)],
        out_specs=[pl.BlockSpec((1, N_HEADS, hd), lambda s, g, pt: (s, 0, 0)),
                   pl.BlockSpec((1, 1, nb), lambda s, g, pt: (s, 0, 0))],
        scratch_shapes=[pltpu.VMEM((nb, hd), F32), pltpu.VMEM((nb, hd), F32)])
    return pl.pallas_call(
        kern, grid_spec=grid_spec,
        out_shape=[jax.ShapeDtypeStruct((dims.dec, N_HEADS, hd), F32), jax.ShapeDtypeStruct((dims.dec, 1, nb), F32)],
        compiler_params=_cparams(("parallel", "arbitrary"), 32), name="nsa_decode_compress",
    )(page_table, q_s, tab_t, *([cache] * pp))


def _dnsa_select_kernel(imp_ref, idx_ref, *, n_past, topk):
    n_seq, width = imp_ref.shape
    blk = lax.broadcasted_iota(jnp.int32, (n_seq, width), 1)
    cur = n_past
    imp = jnp.where((blk == 0) | (blk == cur) | (blk == cur - 1), 2.0, imp_ref[...])
    imp = jnp.where(blk <= cur, imp, -1.0)

    def body(n, rank):
        col = jnp.sum(jnp.where(blk == n, imp, 0.0), axis=1, keepdims=True)
        beats = (col > imp) | ((col == imp) & (blk > n))
        return rank + jnp.where(beats, 1, 0)

    rank = lax.fori_loop(0, cur + 1, body, jnp.zeros((n_seq, width), jnp.int32))
    lane = lax.broadcasted_iota(jnp.int32, (n_seq, V7X_LANES), 1)
    out = jnp.full((n_seq, V7X_LANES), -1, jnp.int32)
    for j in range(topk):
        picked = jnp.sum(jnp.where((rank == j) & (imp >= 0.0), blk + 1, 0), axis=1, keepdims=True) - 1
        out = jnp.where(lane == j, picked, out)
    idx_ref[...] = out


def dnsa_select(imp, dims):
    n_seq, n_past = imp.shape
    width = -(-(n_past + 1) // V7X_LANES) * V7X_LANES
    imp = jnp.pad(imp, ((0, 0), (0, width - n_past)))
    kern = functools.partial(_dnsa_select_kernel, n_past=n_past, topk=min(NSA_TOPK, n_past + 1))
    return pl.pallas_call(
        kern, grid=(1,),
        in_specs=[pl.BlockSpec((n_seq, width), lambda i: (0, 0))],
        out_specs=pl.BlockSpec((n_seq, V7X_LANES), lambda i: (0, 0)),
        out_shape=jax.ShapeDtypeStruct((n_seq, V7X_LANES), jnp.int32),
        compiler_params=_cparams(("arbitrary",), 16), name="nsa_decode_select",
    )(imp)


def _dnsa_attend_kernel(pt_ref, idx_ref, q_ref, new_ref, oc_ref, gate_ref, win_ref, tab_ref, *refs,
                        topk, scale, past, n_past):
    blk_refs = refs[:topk]
    o_ref = refs[topk]
    s_id = pl.program_id(0)
    hd = HEAD_DIM
    q = q_ref[0]
    qf = q.astype(F32)
    new = new_ref[0].astype(F32)
    tab_t = tab_ref[...]
    bias_self = tab_t[:, 0:1]

    def one_key_scores(k_row):
        return jnp.sum(qf * k_row, axis=-1, keepdims=True) * scale + bias_self

    n_keys = topk * NSA_BLOCK
    slot = lax.broadcasted_iota(jnp.int32, (1, n_keys), 1) // NSA_BLOCK
    within = lax.broadcasted_iota(jnp.int32, (1, n_keys), 1) % NSA_BLOCK
    blk_of = jnp.full((1, n_keys), -1, jnp.int32)
    has_new = jnp.zeros((1, 1), jnp.int32)
    for j in range(topk):
        bj = idx_ref[s_id, j]
        blk_of = jnp.where(slot == j, bj, blk_of)
        has_new = jnp.maximum(has_new, jnp.where(bj == n_past, 1, 0))
    valid = (blk_of >= 0) & (blk_of < n_past)
    dist = past - (blk_of * NSA_BLOCK + within)
    ksel = jnp.concatenate([blk_refs[j][0][:, :hd] for j in range(topk)], axis=0).astype(BF16)
    vsel = jnp.concatenate([blk_refs[j][0][:, hd:] for j in range(topk)], axis=0).astype(BF16)
    s = _dot_nt(q, ksel) * scale + _bias_rows(jnp.where(valid, dist, 0), tab_t)
    s = jnp.where(valid, s, NEG_BIG)
    new_ok = has_new > 0
    s_new = jnp.where(new_ok, one_key_scores(new[:, 2 * hd:3 * hd]), NEG_BIG)
    m = jnp.maximum(jnp.max(s, axis=-1, keepdims=True), s_new)
    e = jnp.where(valid, jnp.exp(s - m), 0.0)
    e_new = jnp.where(new_ok, jnp.exp(s_new - m), 0.0)
    l = jnp.sum(e, axis=-1, keepdims=True) + e_new
    o_s = (_dot(e.astype(BF16), vsel) + e_new * new[:, 3 * hd:4 * hd]) / l

    n_win = win_ref.shape[1]
    r = lax.broadcasted_iota(jnp.int32, (1, n_win), 1)
    dist_w = n_win - r
    valid_w = (dist_w < NSA_WINDOW) & (past - dist_w >= 0)
    win = win_ref[0]
    s_w = _dot_nt(q, win[:, :hd].astype(BF16)) * scale + _bias_rows(dist_w, tab_t)
    s_w = jnp.where(valid_w, s_w, NEG_BIG)
    s_wn = one_key_scores(new[:, 4 * hd:5 * hd])
    m_w = jnp.maximum(jnp.max(s_w, axis=-1, keepdims=True), s_wn)
    e_w = jnp.where(valid_w, jnp.exp(s_w - m_w), 0.0)
    e_wn = jnp.exp(s_wn - m_w)
    l_w = jnp.sum(e_w, axis=-1, keepdims=True) + e_wn
    o_w = (_dot(e_w.astype(BF16), win[:, hd:].astype(BF16)) + e_wn * new[:, 5 * hd:6 * hd]) / l_w

    gate = gate_ref[0]
    o_ref[0] = gate[:, 0:1] * oc_ref[0] + gate[:, 1:2] * o_s + gate[:, 2:3] * o_w


def dnsa_attend(q_s, new_rows, o_c, gates_s, win, cache_blocks, page_table, idx, tab_t, dims):
    hd = HEAD_DIM
    n_past = dims.past // NSA_BLOCK
    topk = min(NSA_TOPK, n_past + 1)
    per_page = dims.page // NSA_BLOCK

    def blk_spec(j):
        def index(s, pt, ix):
            b = jnp.clip(ix[s, j], 0, n_past - 1)
            return (pt[s, b // per_page] * per_page + b % per_page, 0, 1)
        return pl.BlockSpec((1, NSA_BLOCK, 2 * hd), index)

    def per_seq(shape):
        return pl.BlockSpec((1,) + shape, lambda s, pt, ix: (s,) + (0,) * len(shape))

    kern = functools.partial(_dnsa_attend_kernel, topk=topk, scale=hd ** -0.5, past=dims.past, n_past=n_past)
    grid_spec = pltpu.PrefetchScalarGridSpec(
        num_scalar_prefetch=2, grid=(dims.dec,),
        in_specs=[per_seq((N_HEADS, hd)), per_seq((1, 6 * hd)), per_seq((N_HEADS, hd)), per_seq((N_HEADS, 3)),
                  per_seq(win.shape[1:]), pl.BlockSpec(tab_t.shape, lambda s, pt, ix: (0, 0))]
                 + [blk_spec(j) for j in range(topk)],
        out_specs=per_seq((N_HEADS, hd)))
    return pl.pallas_call(
        kern, grid_spec=grid_spec, out_shape=jax.ShapeDtypeStruct((dims.dec, N_HEADS, hd), F32),
        compiler_params=_cparams(("parallel",), 32), name="nsa_decode_attend",
    )(page_table, idx, q_s, new_rows, o_c, gates_s, win, tab_t, *([cache_blocks] * topk))


def nsa_mixer(h, x_res, dims, cache, win_state, page_table, rel_table, tiles, w_in, w_o, hg=8):
    hd = HEAD_DIM
    nq = N_HEADS * hd
    nkv = 6 * hd
    q, = dense(h, w_in[:, :nq].astype(BF16), _epi_bf16, [(BF16, nq, min(512, nq))], name="nsa_q")
    kv, kvb = dense(h, w_in[:, nq:nq + nkv].astype(BF16), _epi_f32_bf16, [(F32, nkv, nkv), (BF16, nkv, nkv)], tn=nkv, name="nsa_kv")
    ngate = 3 * N_HEADS
    gates, = dense(h, w_in[:, nq + nkv:].astype(BF16), _epi_sigmoid, [(F32, ngate, ngate)], tn=ngate, name="nsa_gates")
    m = h.shape[0]
    hg = min(hg, N_HEADS)
    gates_g = gates.reshape(m, 3, N_HEADS // hg, hg).transpose(2, 0, 1, 3).reshape(N_HEADS // hg, m, 3 * hg)
    tab_t = rel_table.T
    o_p = pnsa_attention(q, gates_g, kv, kvb, tiles, rel_table, dims, hg=hg)
    q_s = _sample_heads(q, dims, hd)
    n_pool = cache.shape[0]
    o_c, imp = dnsa_compress(q_s, cache.reshape(n_pool, dims.page, 4 * hd), page_table, tab_t, dims)
    idx = dnsa_select(imp.reshape(dims.dec, -1), dims)
    gates_s = gates[dims.n_prompt:].reshape(dims.dec, 3, N_HEADS).transpose(0, 2, 1)
    win = win_state.reshape(dims.dec, win_state.shape[1], 2 * hd)
    o_s = dnsa_attend(q_s, kvb[dims.n_prompt:].reshape(dims.dec, 1, nkv), o_c, gates_s, win,
                      cache.reshape(n_pool * dims.page // NSA_BLOCK, NSA_BLOCK, 4 * hd), page_table, idx, tab_t, dims)
    o = _join(o_p, o_s, dims)
    y, = dense(o, w_o.astype(BF16), _epi_residual, [(F32, w_o.shape[1], min(512, w_o.shape[1]))],
               extras=[(x_res, "tile")], name="nsa_out")
    return y, kv


def kernel(x_prompt, x_sample, cache_nsa_kv, state_nsa_win, cache_mla, cache_diff_kv, cache_sb_kv, page_table,
           rel_table, norm_g, final_g, nsa_w_in, nsa_w_o, mla_w_in, mla_g_q, mla_g_kv, mla_w_uq, mla_w_ukv, mla_w_o,
           diff_w_in, diff_lambda, diff_g_sub, diff_w_o, sb_w_in, sb_w_o, ffn_w_in, ffn_w_out):
    batch, seq, d = x_prompt.shape
    dec = x_sample.shape[0]
    page = cache_sb_kv.shape[1]
    dims = _Dims(batch, seq, dec, page_table.shape[1] * page, page)
    x = jnp.concatenate([x_prompt.reshape(-1, d), x_sample.reshape(-1, d)], axis=0)
    assert x_sample.shape[1] == 1, "the sample group decodes one token per sequence"
    n_p = dims.n_prompt
    tiles = bias_tiles(rel_table)
    rows = {}
    delta = None
    for i in range(norm_g.shape[0]):
        if delta is None:
            h = rms_norm(x, norm_g[i, 0])
        else:
            x, h = add_rms_norm(x, delta, norm_g[i, 0])
        kind = i % 4
        if kind == 0:
            x, rows[0] = nsa_mixer(h, x, dims, cache_nsa_kv, state_nsa_win, page_table, rel_table, tiles, nsa_w_in, nsa_w_o)
        elif kind == 1:
            x, rows[1] = mla_mixer(h, x, dims, cache_mla, page_table, mla_w_in, mla_g_q, mla_g_kv, mla_w_uq, mla_w_ukv, mla_w_o)
        elif kind == 2:
            x, rows[2] = diff_mixer(h, x, dims, cache_diff_kv, page_table, rel_table, tiles, diff_w_in, diff_lambda,
                                    diff_g_sub, diff_w_o)
        else:
            x, rows[3] = sb_mixer(h, x, dims, cache_sb_kv, page_table, sb_w_in, sb_w_o)
        h = rms_norm(x, norm_g[i, 1])
        delta = ffn(h, ffn_w_in[i].astype(BF16), ffn_w_out[i].astype(BF16))
    out = add_final_norm(x, delta, final_g)

    def split(a, tail):
        return a[:n_p].reshape((batch, seq) + tail), a[n_p:].reshape((dec, 1) + tail)

    hd = HEAD_DIM
    y_p, y_s = split(out, (d,))
    nsa_kv_p, nsa_kv_s = split(rows[0][:, :4 * hd], (4, 1, hd))
    win_p, win_s = split(rows[0][:, 4 * hd:], (2, 1, hd))
    nsa_win_p = win_p[:, -NSA_WINDOW:]
    nsa_win_s = jnp.concatenate([state_nsa_win, win_s], axis=1)[:, -state_nsa_win.shape[1]:]
    mla_p, mla_s = split(rows[1], (rows[1].shape[1],))
    diff_p, diff_s = split(rows[2], (2, 1, 2 * DIFF_D))
    sb_p, sb_s = split(rows[3], (2, SB_KV, hd))
    return (y_p, y_s, nsa_kv_p, nsa_kv_s, nsa_win_p, nsa_win_s, mla_p, mla_s, diff_p, diff_s, sb_p, sb_s)
```

```python
import functools
import math

import jax
import jax.numpy as jnp
from jax import lax
from jax.experimental import pallas as pl
from jax.experimental.pallas import tpu as pltpu

BF16 = jnp.bfloat16
F32 = jnp.float32

N_HEADS = 32
HEAD_DIM = 128
QBLOCK = 128
RMS_EPS = 1e-6
REL_BUCKETS = 32
REL_MAX_DIST = 128
NSA_BLOCK = 64
NSA_TOPK = 16
NSA_WINDOW = 512
MLA_NOPE = 128
MLA_ROPE = 64
MLA_V = 128
ROPE_BASE = 10000.0
DIFF_HEADS = 16
DIFF_D = 128
DIFF_LAMBDA_INIT = 0.8 - 0.6 * math.exp(-0.3 * 2)
SB_KV = 2

V7X_VMEM_BYTES = 64 * 1024 * 1024
V7X_LANES = 128
NEG_BIG = -1e30


def _cparams(semantics, vmem_mb):
    return pltpu.CompilerParams(dimension_semantics=semantics, vmem_limit_bytes=vmem_mb * 1024 * 1024)


def _tile(n, target, mult=16):
    best = None
    for t in range(mult, min(n, target) + 1, mult):
        if n % t == 0:
            best = t
    return best if best is not None else n


def _rms(x, g):
    y = x * lax.rsqrt(jnp.mean(x * x, axis=-1, keepdims=True) + RMS_EPS)
    return y * g


def _norm_kernel(x_ref, g_ref, h_ref):
    h_ref[...] = _rms(x_ref[...], g_ref[...]).astype(h_ref.dtype)


def _add_norm_kernel(x_ref, d_ref, g_ref, y_ref, h_ref):
    y = x_ref[...] + d_ref[...]
    y_ref[...] = y
    h_ref[...] = _rms(y, g_ref[...]).astype(h_ref.dtype)


def _add_final_kernel(x_ref, d_ref, g_ref, o_ref):
    o_ref[...] = _rms(x_ref[...] + d_ref[...], g_ref[...])


def rms_norm(x, g):
    m, d = x.shape
    tm = _tile(m, 320)
    row = pl.BlockSpec((tm, d), lambda i: (i, 0))
    return pl.pallas_call(
        _norm_kernel, grid=(m // tm,),
        in_specs=[row, pl.BlockSpec((1, d), lambda i: (0, 0))],
        out_specs=row, out_shape=jax.ShapeDtypeStruct((m, d), BF16),
        compiler_params=_cparams(("parallel",), 32), name="rms_norm",
    )(x, g.reshape(1, d))


def add_rms_norm(x, delta, g):
    m, d = x.shape
    tm = _tile(m, 320)
    row = pl.BlockSpec((tm, d), lambda i: (i, 0))
    return pl.pallas_call(
        _add_norm_kernel, grid=(m // tm,),
        in_specs=[row, row, pl.BlockSpec((1, d), lambda i: (0, 0))],
        out_specs=[row, row],
        out_shape=[jax.ShapeDtypeStruct((m, d), F32), jax.ShapeDtypeStruct((m, d), BF16)],
        compiler_params=_cparams(("parallel",), 48), name="add_rms_norm",
    )(x, delta, g.reshape(1, d))


def add_final_norm(x, delta, g):
    m, d = x.shape
    tm = _tile(m, 320)
    row = pl.BlockSpec((tm, d), lambda i: (i, 0))
    return pl.pallas_call(
        _add_final_kernel, grid=(m // tm,),
        in_specs=[row, row, pl.BlockSpec((1, d), lambda i: (0, 0))],
        out_specs=row, out_shape=jax.ShapeDtypeStruct((m, d), F32),
        compiler_params=_cparams(("parallel",), 48), name="add_final_norm",
    )(x, delta, g.reshape(1, d))


def _dense_kernel(*refs, epilogue, n_extra, n_out):
    a_ref, w_ref = refs[0], refs[1]
    extra = refs[2:2 + n_extra]
    outs = refs[2 + n_extra:2 + n_extra + n_out]
    acc = jnp.dot(a_ref[...], w_ref[...], preferred_element_type=F32)
    epilogue(acc, extra, outs)


def dense(a, w, epilogue, outs, extras=(), tn=512, tm_target=640, name="dense"):
    m, k = a.shape
    n = w.shape[1]
    tn = min(tn, n)
    assert n % tn == 0
    tm = _tile(m, tm_target)
    in_specs = [pl.BlockSpec((tm, k), lambda i, j: (i, 0)), pl.BlockSpec((k, tn), lambda i, j: (0, j))]
    args = [a, w]
    for arr, kind in extras:
        if kind == "tile":
            in_specs.append(pl.BlockSpec((tm, tn), lambda i, j: (i, j)))
        elif kind == "row":
            in_specs.append(pl.BlockSpec((tm, arr.shape[1]), lambda i, j: (i, 0)))
        else:
            in_specs.append(pl.BlockSpec(arr.shape, lambda i, j, nd=arr.ndim: (0,) * nd))
        args.append(arr)
    out_specs, out_shape = [], []
    for dtype, cols, bcols in outs:
        out_specs.append(pl.BlockSpec((tm, bcols), lambda i, j: (i, j)))
        out_shape.append(jax.ShapeDtypeStruct((m, cols), dtype))
    kern = functools.partial(_dense_kernel, epilogue=epilogue, n_extra=len(extras), n_out=len(outs))
    res = pl.pallas_call(
        kern, grid=(m // tm, n // tn), in_specs=in_specs, out_specs=out_specs, out_shape=out_shape,
        compiler_params=_cparams(("parallel", "arbitrary"), 48), name=name,
    )(*args)
    return res


def _epi_bf16(acc, extra, outs):
    outs[0][...] = acc.astype(BF16)


def _epi_f32_bf16(acc, extra, outs):
    outs[0][...] = acc
    outs[1][...] = acc.astype(BF16)


def _epi_sigmoid(acc, extra, outs):
    outs[0][...] = 1.0 / (1.0 + jnp.exp(-acc))


def _epi_residual(acc, extra, outs):
    outs[0][...] = extra[0][...] + acc


def _epi_rms_bf16(acc, extra, outs):
    outs[0][...] = _rms(acc, extra[0][...]).astype(BF16)


def _row_positions(tm, seq, n_prompt, past_len):
    r = pl.program_id(0) * tm + lax.broadcasted_iota(jnp.int32, (tm, 1), 0)
    pos = jnp.where(r < n_prompt, r % seq, past_len)
    return pos.astype(F32)


def _rope_tables(pos, half):
    lane = lax.broadcasted_iota(jnp.int32, (1, V7X_LANES), 1)
    inv_freq = jnp.exp(-math.log(ROPE_BASE) * (lane % half).astype(F32) / half)
    ang = pos * inv_freq
    return jnp.cos(ang), jnp.sin(ang), lane


def _rope_lanes(x, pos, half):
    cos, sin, _ = _rope_tables(pos, half)
    return _rope_apply(x, cos, sin, half)


def _rope_apply(x, cos, sin, half):
    lane = lax.broadcasted_iota(jnp.int32, (1, V7X_LANES), 1)
    first = lane < half
    second = (lane >= half) & (lane < 2 * half)
    x2_at_x1 = pltpu.roll(x, V7X_LANES - half, axis=1)
    x1_at_x2 = pltpu.roll(x, half, axis=1)
    out = jnp.where(first, x * cos - x2_at_x1 * sin, jnp.where(second, x * cos + x1_at_x2 * sin, 0.0))
    return out


def _ffn_kernel(h_ref, wg_ref, wu_ref, wo_ref, o_ref):
    j = pl.program_id(1)
    h = h_ref[...]
    g = jnp.dot(h, wg_ref[...], preferred_element_type=F32)
    u = jnp.dot(h, wu_ref[...], preferred_element_type=F32)
    act = (g * (1.0 / (1.0 + jnp.exp(-g))) * u).astype(BF16)
    part = jnp.dot(act, wo_ref[...], preferred_element_type=F32)

    @pl.when(j == 0)
    def _():
        o_ref[...] = part

    @pl.when(j != 0)
    def _():
        o_ref[...] += part


FFN_HIDDEN_BLOCK = 512


def ffn(h, w_in, w_out, tm_target=416, tf=FFN_HIDDEN_BLOCK):
    m, d = h.shape
    f = w_out.shape[0]
    f_pad = -(-f // tf) * tf
    wg = jnp.pad(w_in[:, :f], ((0, 0), (0, f_pad - f))).astype(BF16)
    wu = jnp.pad(w_in[:, f:], ((0, 0), (0, f_pad - f))).astype(BF16)
    wo = jnp.pad(w_out, ((0, f_pad - f), (0, 0))).astype(BF16)
    tm = _tile(m, tm_target)
    return pl.pallas_call(
        _ffn_kernel, grid=(m // tm, f_pad // tf),
        in_specs=[pl.BlockSpec((tm, d), lambda i, j: (i, 0)),
                  pl.BlockSpec((d, tf), lambda i, j: (0, j)),
                  pl.BlockSpec((d, tf), lambda i, j: (0, j)),
                  pl.BlockSpec((tf, d), lambda i, j: (j, 0))],
        out_specs=pl.BlockSpec((tm, d), lambda i, j: (i, 0)),
        out_shape=jax.ShapeDtypeStruct((m, d), F32),
        compiler_params=_cparams(("parallel", "arbitrary"), 52), name="ffn",
    )(h, wg, wu, wo)


class _Dims:
    def __init__(self, batch, seq, dec, past, page):
        self.batch, self.seq, self.dec, self.past, self.page = batch, seq, dec, past, page
        self.n_prompt = batch * seq
        self.n_pages = past // page
        self.qblocks = seq // QBLOCK
        assert seq % QBLOCK == 0 and past % page == 0 and page % NSA_BLOCK == 0


def _dot_nt(a, b):
    return lax.dot_general(a, b, (((1,), (1,)), ((), ())), preferred_element_type=F32)


def _dot(a, b):
    return jnp.dot(a, b, preferred_element_type=F32)


def _dot_f32_by_01(x, u):
    hi = x.astype(BF16)
    r1 = x - hi.astype(F32)
    mid = r1.astype(BF16)
    lo = (r1 - mid.astype(F32)).astype(BF16)
    return _dot(hi, u) + _dot(mid, u) + _dot(lo, u)


def _stack_heads(q_ref, qs_ref, n_heads, stride=HEAD_DIM, offset=0, width=HEAD_DIM):
    for h in range(n_heads):
        lo = offset + h * stride
        qs_ref[h * QBLOCK:(h + 1) * QBLOCK, :] = q_ref[:, lo:lo + width]


def _strict_upper(n):
    r = lax.broadcasted_iota(jnp.int32, (n, n), 0)
    c = lax.broadcasted_iota(jnp.int32, (n, n), 1)
    return jnp.where(r > c, 1.0, 0.0).astype(BF16)


def _log_sigmoid_pair(z):
    t = jnp.log1p(jnp.exp(-jnp.abs(z)))
    return jnp.minimum(z, 0.0) - t, -jnp.maximum(z, 0.0) - t


def _psb_kernel(q_ref, k_ref, v_ref, o_ref, qs_ref, acc_ref, carry_ref, *, hg, scale):
    qb = pl.program_id(1)
    rows = hg * QBLOCK
    _stack_heads(q_ref, qs_ref, hg)
    acc_ref[...] = jnp.zeros_like(acc_ref)
    carry_ref[...] = jnp.zeros_like(carry_ref)
    t_row = lax.broadcasted_iota(jnp.int32, (rows, QBLOCK), 0) % QBLOCK
    lane = lax.broadcasted_iota(jnp.int32, (rows, QBLOCK), 1)
    rel = lane - t_row
    upper = _strict_upper(QBLOCK)

    def body(i, carry):
        off = pl.multiple_of((qb - i) * QBLOCK, QBLOCK)
        kc = k_ref[pl.ds(off, QBLOCK), :]
        vc = v_ref[pl.ds(off, QBLOCK), :]
        z = _dot_nt(qs_ref[...], kc) * scale
        valid = rel < i * QBLOCK
        ls_pos, ls_neg = _log_sigmoid_pair(z)
        log_keep = jnp.where(valid, ls_neg, 0.0)
        later = _dot_f32_by_01(log_keep, upper) + carry_ref[...]
        a = jnp.where(valid, jnp.exp(ls_pos + later), 0.0)
        acc_ref[...] += _dot(a.astype(BF16), vc)
        carry_ref[...] += jnp.sum(log_keep, axis=-1, keepdims=True)
        return carry

    lax.fori_loop(0, qb + 1, body, 0)
    for h in range(hg):
        o_ref[:, h * HEAD_DIM:(h + 1) * HEAD_DIM] = acc_ref[h * QBLOCK:(h + 1) * QBLOCK, :].astype(o_ref.dtype)


def psb_attention(q, kvb, dims, hg=8):
    nq = N_HEADS * HEAD_DIM
    per_group = N_HEADS // SB_KV
    hg = min(hg, per_group)
    n_hg = N_HEADS // hg
    qbs, seq = dims.qblocks, dims.seq
    kern = functools.partial(_psb_kernel, hg=hg, scale=HEAD_DIM ** -0.5)
    return pl.pallas_call(
        kern, grid=(dims.batch, qbs, n_hg),
        in_specs=[pl.BlockSpec((QBLOCK, hg * HEAD_DIM), lambda b, i, g: (b * qbs + i, g)),
                  pl.BlockSpec((seq, HEAD_DIM), lambda b, i, g: (b, (g * hg) // per_group)),
                  pl.BlockSpec((seq, HEAD_DIM), lambda b, i, g: (b, SB_KV + (g * hg) // per_group))],
        out_specs=pl.BlockSpec((QBLOCK, hg * HEAD_DIM), lambda b, i, g: (b * qbs + i, g)),
        out_shape=jax.ShapeDtypeStruct((dims.n_prompt, nq), BF16),
        scratch_shapes=[pltpu.VMEM((hg * QBLOCK, HEAD_DIM), BF16),
                        pltpu.VMEM((hg * QBLOCK, HEAD_DIM), F32),
                        pltpu.VMEM((hg * QBLOCK, V7X_LANES), F32)],
        compiler_params=_cparams(("parallel", "parallel", "arbitrary"), 32), name="sb_prompt_attention",
    )(q, kvb, kvb)


def _flat_cache(cache, rows):
    return cache.reshape(-1, rows * (math.prod(cache.shape[2:]) // V7X_LANES), V7X_LANES)


def _page_channel(page_refs, order, n_chan, chan):
    rows = page_refs[0].shape[1] // n_chan
    return jnp.concatenate([page_refs[i][0, pl.ds(chan, rows, stride=n_chan), :] for i in order], axis=0)


def _dsb_kernel(pt_ref, q_ref, upper_ref, *refs, pp, scale):
    page_refs = refs[:pp]
    o_ref, acc_ref, carry_ref = refs[pp:]
    step = pl.program_id(1)
    per_group = N_HEADS // SB_KV

    @pl.when(step == 0)
    def _():
        acc_ref[...] = jnp.zeros_like(acc_ref)
        carry_ref[...] = jnp.zeros_like(carry_ref)

    order = range(pp - 1, -1, -1)
    n_chan = 2 * SB_KV
    q = q_ref[0]
    k = [_page_channel(page_refs, order, n_chan, g).astype(BF16) for g in range(SB_KV)]
    v = [_page_channel(page_refs, order, n_chan, SB_KV + g).astype(BF16) for g in range(SB_KV)]
    z = jnp.concatenate([_dot_nt(q[g * per_group:(g + 1) * per_group], k[g]) for g in range(SB_KV)], axis=0) * scale
    ls_pos, ls_neg = _log_sigmoid_pair(z)
    later = _dot_f32_by_01(ls_neg, upper_ref[...]) + carry_ref[...]
    a = jnp.exp(ls_pos + later).astype(BF16)
    acc_ref[...] += jnp.concatenate([_dot(a[g * per_group:(g + 1) * per_group], v[g]) for g in range(SB_KV)], axis=0)
    carry_ref[...] += jnp.sum(ls_neg, axis=-1, keepdims=True)

    @pl.when(step == pl.num_programs(1) - 1)
    def _():
        o_ref[0] = acc_ref[...]


def dsb_attention(q_s, cache, page_table, dims, pp=8):
    assert HEAD_DIM == V7X_LANES
    n_pages = dims.n_pages
    pp = min(pp, n_pages)
    assert n_pages % pp == 0
    n_keys = pp * dims.page
    r = lax.broadcasted_iota(jnp.int32, (n_keys, n_keys), 0)
    c = lax.broadcasted_iota(jnp.int32, (n_keys, n_keys), 1)
    upper = (r > c).astype(BF16)

    def page_spec(i):
        return pl.BlockSpec((1,) + cache.shape[1:], lambda s, g, pt: (pt[s, n_pages - 1 - (g * pp + i)], 0, 0))

    kern = functools.partial(_dsb_kernel, pp=pp, scale=HEAD_DIM ** -0.5)
    grid_spec = pltpu.PrefetchScalarGridSpec(
        num_scalar_prefetch=1, grid=(dims.dec, n_pages // pp),
        in_specs=[pl.BlockSpec((1, N_HEADS, HEAD_DIM), lambda s, g, pt: (s, 0, 0)),
                  pl.BlockSpec((n_keys, n_keys), lambda s, g, pt: (0, 0))] + [page_spec(i) for i in range(pp)],
        out_specs=pl.BlockSpec((1, N_HEADS, HEAD_DIM), lambda s, g, pt: (s, 0, 0)),
        scratch_shapes=[pltpu.VMEM((N_HEADS, HEAD_DIM), F32), pltpu.VMEM((N_HEADS, 1), F32)])
    return pl.pallas_call(
        kern, grid_spec=grid_spec, out_shape=jax.ShapeDtypeStruct((dims.dec, N_HEADS, HEAD_DIM), F32),
        compiler_params=_cparams(("parallel", "arbitrary"), 40), name="sb_decode_attention",
    )(page_table, q_s, upper, *([cache] * pp))


def _sample_heads(x, dims, width):
    return x[dims.n_prompt:].reshape(dims.dec, -1, width)


def _join(o_prompt, o_sample, dims):
    return jnp.concatenate([o_prompt, o_sample.reshape(dims.dec, -1).astype(o_prompt.dtype)], axis=0)


def sb_mixer(h, x_res, dims, cache, page_table, w_in, w_o):
    nq = N_HEADS * HEAD_DIM
    q, = dense(h, w_in[:, :nq].astype(BF16), _epi_bf16, [(BF16, nq, min(512, nq))], name="sb_q")
    nkv = w_in.shape[1] - nq
    kv, kvb = dense(h, w_in[:, nq:].astype(BF16), _epi_f32_bf16, [(F32, nkv, nkv), (BF16, nkv, nkv)], tn=nkv, name="sb_kv")
    o_p = psb_attention(q, kvb, dims)
    o_s = dsb_attention(_sample_heads(q, dims, HEAD_DIM), _flat_cache(cache, dims.page), page_table, dims)
    o = _join(o_p, o_s, dims)
    y, = dense(o, w_o.astype(BF16), _epi_residual, [(F32, w_o.shape[1], min(512, w_o.shape[1]))],
               extras=[(x_res, "tile")], name="sb_out")
    return y, kv


def _rel_bucket(dist):
    max_exact = REL_BUCKETS // 2
    d = jnp.maximum(dist, 0)
    log_ratio = jnp.log(jnp.maximum(d, 1).astype(F32) / max_exact) / math.log(REL_MAX_DIST / max_exact)
    far = max_exact + (log_ratio * (REL_BUCKETS - max_exact)).astype(jnp.int32)
    return jnp.where(d < max_exact, d, jnp.minimum(far, REL_BUCKETS - 1))


def _bias_rows(dist, tab_t):
    bucket = _rel_bucket(dist)
    out = jnp.zeros((tab_t.shape[0], dist.shape[1]), F32)
    for b in range(REL_BUCKETS):
        out = jnp.where(bucket == b, tab_t[:, b:b + 1], out)
    return out


def _bias_tiles_kernel(tab_ref, o_ref):
    c = pl.program_id(0)
    i = lax.broadcasted_iota(jnp.int32, (QBLOCK, QBLOCK), 0)
    j = lax.broadcasted_iota(jnp.int32, (QBLOCK, QBLOCK), 1)
    for cls in range(3):
        bucket = _rel_bucket(i - j + cls * QBLOCK)
        val = jnp.zeros((QBLOCK, QBLOCK), F32)
        for b in range(REL_BUCKETS):
            val = jnp.where(bucket == b, tab_ref[b, c], val)
        o_ref[cls] = val


def bias_tiles(rel_table):
    assert REL_MAX_DIST <= QBLOCK + 1
    n_cols = rel_table.shape[1]
    return pl.pallas_call(
        _bias_tiles_kernel, grid=(n_cols,),
        in_specs=[pl.BlockSpec(memory_space=pltpu.SMEM)],
        out_specs=pl.BlockSpec((3, QBLOCK, QBLOCK), lambda c: (0, c, 0)),
        out_shape=jax.ShapeDtypeStruct((3, n_cols * QBLOCK, QBLOCK), F32),
        compiler_params=_cparams(("parallel",), 16), name="rel_bias_tiles",
    )(rel_table)


def _online_softmax_step(s, vc, m_ref, l_ref, acc_ref, mask=None, v_transposed=False):
    m_old = m_ref[...]
    m_new = jnp.maximum(m_old, jnp.max(s, axis=-1, keepdims=True))
    alpha = jnp.exp(m_old - m_new)
    p = jnp.exp(s - _lanes(m_new, s.shape[1]))
    if mask is not None:
        p = jnp.where(mask, p, 0.0)
    l_ref[...] = alpha * l_ref[...] + jnp.sum(p, axis=-1, keepdims=True)
    pv = _dot_nt(p.astype(BF16), vc) if v_transposed else _dot(p.astype(BF16), vc)
    acc_ref[...] = _lanes(alpha, pv.shape[1]) * acc_ref[...] + pv
    m_ref[...] = m_new


def _lanes(x, width):
    if x.shape[1] == 1 or x.shape[1] == width:
        return x
    assert width % x.shape[1] == 0
    return jnp.concatenate([x] * (width // x.shape[1]), axis=1)


def _softmax_init(m_ref, l_ref, acc_ref):
    m_ref[...] = jnp.full(m_ref.shape, NEG_BIG, F32)
    l_ref[...] = jnp.zeros_like(l_ref)
    acc_ref[...] = jnp.zeros_like(acc_ref)


def _causal_rel(rows):
    t_row = lax.broadcasted_iota(jnp.int32, (rows, QBLOCK), 0) % QBLOCK
    lane = lax.broadcasted_iota(jnp.int32, (rows, QBLOCK), 1)
    return lane - t_row


def _diff_lambda(lam_ref):
    lp = lam_ref[...]
    return (jnp.exp(jnp.sum(lp[0:1] * lp[1:2], axis=-1, keepdims=True))
            - jnp.exp(jnp.sum(lp[2:3] * lp[3:4], axis=-1, keepdims=True)) + DIFF_LAMBDA_INIT)


def _diff_finish(o0, o1, lam, g_sub):
    o = o0 - lam * o1
    return _rms(o, g_sub) * (1.0 - DIFF_LAMBDA_INIT)


def _pdiff_kernel(q_ref, k_ref, v_ref, bias_ref, lam_ref, g_ref, o_ref, qs_ref, m_ref, l_ref, acc_ref, *, hg, scale):
    qb = pl.program_id(1)
    rows = hg * QBLOCK
    d = DIFF_D
    for i in range(2):
        _stack_heads(q_ref, qs_ref.at[i], hg, stride=2 * d, offset=i * d)
        _softmax_init(m_ref.at[i], l_ref.at[i], acc_ref.at[i])
    rel = _causal_rel(rows)

    def body(c, carry):
        off = pl.multiple_of(c * QBLOCK, QBLOCK)
        kc = k_ref[pl.ds(off, QBLOCK), :]
        vc = v_ref[pl.ds(off, QBLOCK), :]
        cls = jnp.minimum(qb - c, 2)
        valid = rel <= (qb - c) * QBLOCK
        for i in range(2):
            s = _dot_nt(qs_ref[i], kc[:, i * d:(i + 1) * d]) * scale + bias_ref[cls, i]
            s = jnp.where(valid, s, NEG_BIG)
            _online_softmax_step(s, vc, m_ref.at[i], l_ref.at[i], acc_ref.at[i])
        return carry

    lax.fori_loop(0, qb + 1, body, 0)
    res = _diff_finish(acc_ref[0] / _lanes(l_ref[0], 2 * d), acc_ref[1] / _lanes(l_ref[1], 2 * d),
                       _diff_lambda(lam_ref), g_ref[...])
    for h in range(hg):
        o_ref[:, h * 2 * d:(h + 1) * 2 * d] = res[h * QBLOCK:(h + 1) * QBLOCK, :].astype(o_ref.dtype)


def pdiff_attention(q, kvb, bias, lam_p, g_sub, dims, hg=4):
    d = DIFF_D
    hg = min(hg, DIFF_HEADS)
    qbs, seq = dims.qblocks, dims.seq
    rows = hg * QBLOCK
    kern = functools.partial(_pdiff_kernel, hg=hg, scale=d ** -0.5)
    return pl.pallas_call(
        kern, grid=(dims.batch, qbs, DIFF_HEADS // hg),
        in_specs=[pl.BlockSpec((QBLOCK, hg * 2 * d), lambda b, i, g: (b * qbs + i, g)),
                  pl.BlockSpec((seq, 2 * d), lambda b, i, g: (b, 0)),
                  pl.BlockSpec((seq, 2 * d), lambda b, i, g: (b, 1)),
                  pl.BlockSpec((3, 2, rows, QBLOCK), lambda b, i, g: (0, 0, g, 0)),
                  pl.BlockSpec((4, d), lambda b, i, g: (0, 0)),
                  pl.BlockSpec((1, 2 * d), lambda b, i, g: (0, 0))],
        out_specs=pl.BlockSpec((QBLOCK, hg * 2 * d), lambda b, i, g: (b * qbs + i, g)),
        out_shape=jax.ShapeDtypeStruct((dims.n_prompt, DIFF_HEADS * 2 * d), BF16),
        scratch_shapes=[pltpu.VMEM((2, rows, d), BF16), pltpu.VMEM((2, rows, V7X_LANES), F32),
                        pltpu.VMEM((2, rows, V7X_LANES), F32), pltpu.VMEM((2, rows, 2 * d), F32)],
        compiler_params=_cparams(("parallel", "parallel", "arbitrary"), 32), name="diff_prompt_attention",
    )(q, kvb, kvb, bias, lam_p, g_sub.reshape(1, -1))


def _merge_new_row(s_new, v_new, m_ref, l_ref, acc_ref):
    m_old = m_ref[...]
    m_new = jnp.maximum(m_old, s_new)
    alpha = jnp.exp(m_old - m_new)
    p = jnp.exp(s_new - m_new)
    l_ref[...] = alpha * l_ref[...] + p
    acc_ref[...] = alpha * acc_ref[...] + p * v_new
    m_ref[...] = m_new


def _ddiff_kernel(pt_ref, q_ref, new_ref, tab_ref, lam_ref, g_ref, *refs, pp, scale, past):
    page_refs = refs[:pp]
    o_ref, m_ref, l_ref, acc_ref = refs[pp:]
    step = pl.program_id(1)
    d = DIFF_D
    n_rows = 2 * DIFF_HEADS

    @pl.when(step == 0)
    def _():
        _softmax_init(m_ref, l_ref, acc_ref)

    q = q_ref[0]
    order = range(pp)
    n_chan = 4
    k1, k2, v_lo, v_hi = [_page_channel(page_refs, order, n_chan, c).astype(BF16) for c in range(n_chan)]
    n_keys = k1.shape[0]
    tab_t = tab_ref[...]
    first_key = step * n_keys
    dist = past - (first_key + lax.broadcasted_iota(jnp.int32, (1, n_keys), 1))
    bias = lax.cond(past - (first_key + n_keys - 1) >= QBLOCK,
                    lambda: jnp.broadcast_to(tab_t[:, REL_BUCKETS - 1:REL_BUCKETS], (n_rows, n_keys)),
                    lambda: _bias_rows(dist, tab_t))
    s = jnp.concatenate([_dot_nt(q[:DIFF_HEADS], k1), _dot_nt(q[DIFF_HEADS:], k2)], axis=0) * scale + bias
    _online_softmax_step(s, jnp.concatenate([v_lo, v_hi], axis=1), m_ref, l_ref, acc_ref)

    @pl.when(step == pl.num_programs(1) - 1)
    def _():
        new = new_ref[0].astype(F32)
        qf = q.astype(F32)
        first = lax.broadcasted_iota(jnp.int32, (n_rows, 1), 0) < DIFF_HEADS
        s1 = jnp.sum(qf * new[:, :d], axis=-1, keepdims=True)
        s2 = jnp.sum(qf * new[:, d:2 * d], axis=-1, keepdims=True)
        s_new = jnp.where(first, s1, s2) * scale + tab_t[:, 0:1]
        _merge_new_row(s_new, new[:, 2 * d:], m_ref, l_ref, acc_ref)
        on = acc_ref[...] / l_ref[...]
        o_ref[0] = _diff_finish(on[:DIFF_HEADS], on[DIFF_HEADS:], _diff_lambda(lam_ref), g_ref[...])


def ddiff_attention(q_s, new_rows, cache, page_table, tab_t, lam_p, g_sub, dims, pp=8):
    d = DIFF_D
    n_pages = dims.n_pages
    pp = min(pp, n_pages)
    assert n_pages % pp == 0
    n_rows = 2 * DIFF_HEADS

    def page_spec(i):
        return pl.BlockSpec((1,) + cache.shape[1:], lambda s, g, pt: (pt[s, g * pp + i], 0, 0))

    def whole(shape):
        return pl.BlockSpec(shape, lambda s, g, pt: (0,) * len(shape))

    kern = functools.partial(_ddiff_kernel, pp=pp, scale=d ** -0.5, past=dims.past)
    grid_spec = pltpu.PrefetchScalarGridSpec(
        num_scalar_prefetch=1, grid=(dims.dec, n_pages // pp),
        in_specs=[pl.BlockSpec((1, n_rows, d), lambda s, g, pt: (s, 0, 0)),
                  pl.BlockSpec((1, 1, 4 * d), lambda s, g, pt: (s, 0, 0)),
                  whole((n_rows, REL_BUCKETS)), whole((4, d)), whole((1, 2 * d))]
                 + [page_spec(i) for i in range(pp)],
        out_specs=pl.BlockSpec((1, DIFF_HEADS, 2 * d), lambda s, g, pt: (s, 0, 0)),
        scratch_shapes=[pltpu.VMEM((n_rows, 1), F32), pltpu.VMEM((n_rows, 1), F32), pltpu.VMEM((n_rows, 2 * d), F32)])
    return pl.pallas_call(
        kern, grid_spec=grid_spec, out_shape=jax.ShapeDtypeStruct((dims.dec, DIFF_HEADS, 2 * d), F32),
        compiler_params=_cparams(("parallel", "arbitrary"), 32), name="diff_decode_attention",
    )(page_table, q_s, new_rows, tab_t, lam_p, g_sub.reshape(1, -1), *([cache] * pp))


def diff_mixer(h, x_res, dims, cache, page_table, rel_table, tiles, w_in, lam_p, g_sub, w_o):
    d = DIFF_D
    nq = DIFF_HEADS * 2 * d
    q, = dense(h, w_in[:, :nq].astype(BF16), _epi_bf16, [(BF16, nq, min(512, nq))], name="diff_q")
    nkv = w_in.shape[1] - nq
    kv, kvb = dense(h, w_in[:, nq:].astype(BF16), _epi_f32_bf16, [(F32, nkv, nkv), (BF16, nkv, nkv)], tn=nkv, name="diff_kv")
    bias = tiles.reshape(3, DIFF_HEADS, 2, QBLOCK, QBLOCK).transpose(0, 2, 1, 3, 4).reshape(3, 2, DIFF_HEADS * QBLOCK, QBLOCK)
    o_p = pdiff_attention(q, kvb, bias, lam_p, g_sub, dims)
    q_s = q[dims.n_prompt:].reshape(dims.dec, DIFF_HEADS, 2, d).transpose(0, 2, 1, 3).reshape(dims.dec, 2 * DIFF_HEADS, d)
    tab_t = rel_table.T.reshape(DIFF_HEADS, 2, REL_BUCKETS).transpose(1, 0, 2).reshape(2 * DIFF_HEADS, REL_BUCKETS)
    o_s = ddiff_attention(q_s, kvb[dims.n_prompt:].reshape(dims.dec, 1, nkv), _flat_cache(cache, dims.page),
                          page_table, tab_t, lam_p, g_sub, dims)
    o = _join(o_p, o_s, dims)
    y, = dense(o, w_o.astype(BF16), _epi_residual, [(F32, w_o.shape[1], min(512, w_o.shape[1]))],
               extras=[(x_res, "tile")], name="diff_out")
    return y, kv


MLA_PAD = 128


def _mla_kv_epilogue(acc, extra, outs, *, rank, seq, n_prompt, past):
    tm = acc.shape[0]
    pos = _row_positions(tm, seq, n_prompt, past)
    ckv = _rms(acc[:, :rank], extra[0][...])
    pe = _rope_lanes(acc[:, rank:rank + MLA_PAD], pos, MLA_ROPE // 2)
    outs[0][:, :rank] = ckv
    outs[0][:, rank:rank + MLA_ROPE] = pe[:, :MLA_ROPE]
    outs[1][:, :rank] = ckv.astype(BF16)
    outs[1][:, rank:] = pe.astype(BF16)


def _mla_q_kernel(cq_ref, wn_ref, wp_ref, wk_ref, o_ref, cos_ref, sin_ref, *, rank, seq, n_prompt, past):
    tm = cq_ref.shape[0]
    half = MLA_ROPE // 2

    @pl.when(pl.program_id(1) == 0)
    def _():
        cos, sin, _ = _rope_tables(_row_positions(tm, seq, n_prompt, past), half)
        cos_ref[...] = cos
        sin_ref[...] = sin

    cq = cq_ref[...]
    q_nope = _dot(cq, wn_ref[...]).astype(BF16)
    q_lat = _dot(q_nope, wk_ref[0])
    q_pe = _rope_apply(_dot(cq, wp_ref[...]), cos_ref[...], sin_ref[...], half)
    o_ref[0, :, :rank] = q_lat.astype(BF16)
    o_ref[0, :, rank:] = q_pe.astype(BF16)


def mla_queries(c_q, w_nope, w_pe, wk_t, dims):
    m, qr = c_q.shape
    rank = wk_t.shape[2]
    tm = _tile(m, 640)
    kern = functools.partial(_mla_q_kernel, rank=rank, seq=dims.seq, n_prompt=dims.n_prompt, past=dims.past)
    return pl.pallas_call(
        kern, grid=(m // tm, N_HEADS),
        in_specs=[pl.BlockSpec((tm, qr), lambda i, h: (i, 0)),
                  pl.BlockSpec((qr, MLA_NOPE), lambda i, h: (0, h)),
                  pl.BlockSpec((qr, MLA_PAD), lambda i, h: (0, h)),
                  pl.BlockSpec((1, MLA_NOPE, rank), lambda i, h: (h, 0, 0))],
        out_specs=pl.BlockSpec((1, tm, rank + MLA_PAD), lambda i, h: (h, i, 0)),
        out_shape=jax.ShapeDtypeStruct((N_HEADS, m, rank + MLA_PAD), BF16),
        scratch_shapes=[pltpu.VMEM((tm, MLA_PAD), F32), pltpu.VMEM((tm, MLA_PAD), F32)],
        compiler_params=_cparams(("parallel", "arbitrary"), 32), name="mla_queries",
    )(c_q, w_nope, w_pe, wk_t)


def _pmla_kernel(q_ref, k_ref, wv_ref, o_ref, m_ref, l_ref, acc_ref, *, hg, rank, scale, chunk):
    qb = pl.program_id(1)
    rows = hg * QBLOCK
    _softmax_init(m_ref, l_ref, acc_ref)
    t_row = lax.broadcasted_iota(jnp.int32, (rows, chunk), 0) % QBLOCK
    rel = lax.broadcasted_iota(jnp.int32, (rows, chunk), 1) - t_row
    qs = q_ref[...].reshape(rows, q_ref.shape[2])

    def body(c, carry):
        off = pl.multiple_of(c * chunk, chunk)
        kc = k_ref[pl.ds(off, chunk), :]
        s = _dot_nt(qs, kc) * scale
        s = jnp.where(rel <= qb * QBLOCK - c * chunk, s, NEG_BIG)
        _online_softmax_step(s, kc[:, :rank], m_ref, l_ref, acc_ref)
        return carry

    lax.fori_loop(0, (qb * QBLOCK + QBLOCK + chunk - 1) // chunk, body, 0)
    o_lat = (acc_ref[...] / _lanes(l_ref[...], rank)).astype(BF16)
    for h in range(hg):
        o_ref[:, h * MLA_V:(h + 1) * MLA_V] = _dot(o_lat[h * QBLOCK:(h + 1) * QBLOCK], wv_ref[h]).astype(o_ref.dtype)


def pmla_attention(q_cat, kb, wv, dims, hg=8):
    width = kb.shape[1]
    rank = wv.shape[1]
    hg = min(hg, N_HEADS)
    qbs, seq = dims.qblocks, dims.seq
    rows = hg * QBLOCK
    chunk = 2 * QBLOCK if seq % (2 * QBLOCK) == 0 else QBLOCK
    kern = functools.partial(_pmla_kernel, hg=hg, rank=rank, scale=(MLA_NOPE + MLA_ROPE) ** -0.5, chunk=chunk)
    return pl.pallas_call(
        kern, grid=(dims.batch, qbs, N_HEADS // hg),
        in_specs=[pl.BlockSpec((hg, QBLOCK, width), lambda b, i, g: (g, b * qbs + i, 0)),
                  pl.BlockSpec((seq, width), lambda b, i, g: (b, 0)),
                  pl.BlockSpec((hg, rank, MLA_V), lambda b, i, g: (g, 0, 0))],
        out_specs=pl.BlockSpec((QBLOCK, hg * MLA_V), lambda b, i, g: (b * qbs + i, g)),
        out_shape=jax.ShapeDtypeStruct((dims.n_prompt, N_HEADS * MLA_V), BF16),
        scratch_shapes=[pltpu.VMEM((rows, V7X_LANES), F32), pltpu.VMEM((rows, V7X_LANES), F32),
                        pltpu.VMEM((rows, rank), F32)],
        compiler_params=_cparams(("parallel", "parallel", "arbitrary"), 40), name="mla_prompt_attention",
    )(q_cat, kb, wv)


def _dmla_kernel(pt_ref, q_ref, new_ref, *refs, pp, rank, scale):
    page_refs = refs[:pp]
    o_ref, m_ref, l_ref, acc_ref = refs[pp:]
    step = pl.program_id(1)

    @pl.when(step == 0)
    def _():
        _softmax_init(m_ref, l_ref, acc_ref)

    q = q_ref[0]
    kt = jnp.concatenate([page_refs[i][0].astype(BF16) for i in range(pp)], axis=1)
    pad = jnp.zeros((q.shape[1] - kt.shape[0], kt.shape[1]), BF16)
    s = _dot(q, jnp.concatenate([kt, pad], axis=0)) * scale
    _online_softmax_step(s, kt[:rank], m_ref, l_ref, acc_ref, v_transposed=True)

    @pl.when(step == pl.num_programs(1) - 1)
    def _():
        new = new_ref[0].astype(F32)
        s_new = jnp.sum(q.astype(F32) * new, axis=-1, keepdims=True) * scale
        _merge_new_row(s_new, new[:, :rank], m_ref, l_ref, acc_ref)
        o_ref[0] = (acc_ref[...] / l_ref[...]).astype(o_ref.dtype)


def dmla_attention(q_s, new_rows, cache_t, page_table, dims, pp=8):
    width = q_s.shape[2]
    rank = width - MLA_PAD
    n_pages = dims.n_pages
    pp = min(pp, n_pages)
    assert n_pages % pp == 0

    def page_spec(i):
        return pl.BlockSpec((1,) + cache_t.shape[1:], lambda s, g, pt: (pt[s, g * pp + i], 0, 0))

    kern = functools.partial(_dmla_kernel, pp=pp, rank=rank, scale=(MLA_NOPE + MLA_ROPE) ** -0.5)
    grid_spec = pltpu.PrefetchScalarGridSpec(
        num_scalar_prefetch=1, grid=(dims.dec, n_pages // pp),
        in_specs=[pl.BlockSpec((1, N_HEADS, width), lambda s, g, pt: (s, 0, 0)),
                  pl.BlockSpec((1, 1, width), lambda s, g, pt: (s, 0, 0))] + [page_spec(i) for i in range(pp)],
        out_specs=pl.BlockSpec((1, N_HEADS, rank), lambda s, g, pt: (s, 0, 0)),
        scratch_shapes=[pltpu.VMEM((N_HEADS, 1), F32), pltpu.VMEM((N_HEADS, 1), F32), pltpu.VMEM((N_HEADS, rank), F32)])
    return pl.pallas_call(
        kern, grid_spec=grid_spec, out_shape=jax.ShapeDtypeStruct((dims.dec, N_HEADS, rank), BF16),
        compiler_params=_cparams(("parallel", "arbitrary"), 32), name="mla_decode_attention",
    )(page_table, q_s, new_rows, *([cache_t] * pp))


def _mla_up_kernel(o_ref, wv_ref, out_ref):
    out_ref[...] = _dot(o_ref[0], wv_ref[0]).astype(out_ref.dtype)


def mla_up(o_lat_t, wv):
    n_h, s, rank = o_lat_t.shape
    return pl.pallas_call(
        _mla_up_kernel, grid=(n_h,),
        in_specs=[pl.BlockSpec((1, s, rank), lambda h: (h, 0, 0)), pl.BlockSpec((1, rank, MLA_V), lambda h: (h, 0, 0))],
        out_specs=pl.BlockSpec((s, MLA_V), lambda h: (0, h)),
        out_shape=jax.ShapeDtypeStruct((s, n_h * MLA_V), BF16),
        compiler_params=_cparams(("parallel",), 16), name="mla_value_up",
    )(o_lat_t, wv)


def mla_mixer(h, x_res, dims, cache, page_table, w_in, g_q, g_kv, w_uq, w_ukv, w_o):
    rank = w_ukv.shape[0]
    q_rank = w_uq.shape[0]
    half = MLA_ROPE // 2
    c_q, = dense(h, w_in[:, :q_rank].astype(BF16), _epi_rms_bf16, [(BF16, q_rank, q_rank)],
                 extras=[(g_q.reshape(1, -1), "full")], tn=q_rank, name="mla_cq")
    w_kv = jnp.pad(w_in[:, q_rank:], ((0, 0), (0, MLA_PAD - MLA_ROPE))).astype(BF16)
    epi = functools.partial(_mla_kv_epilogue, rank=rank, seq=dims.seq, n_prompt=dims.n_prompt, past=dims.past)
    rows, kb = dense(h, w_kv, epi, [(F32, rank + MLA_ROPE, rank + MLA_ROPE), (BF16, rank + MLA_PAD, rank + MLA_PAD)],
                     extras=[(g_kv.reshape(1, -1), "full")], tn=rank + MLA_PAD, name="mla_kv")
    w_uq3 = w_uq.reshape(q_rank, N_HEADS, MLA_NOPE + MLA_ROPE)
    w_nope = w_uq3[:, :, :MLA_NOPE].reshape(q_rank, -1).astype(BF16)
    w_pe = jnp.pad(w_uq3[:, :, MLA_NOPE:], ((0, 0), (0, 0), (0, MLA_PAD - MLA_ROPE))).reshape(q_rank, -1).astype(BF16)
    wk_t = jnp.transpose(w_ukv[:, :, :MLA_NOPE], (1, 2, 0)).astype(BF16)
    wv = jnp.transpose(w_ukv[:, :, MLA_NOPE:], (1, 0, 2)).astype(BF16)
    q_cat = mla_queries(c_q, w_nope, w_pe, wk_t, dims)
    o_p = pmla_attention(q_cat, kb, wv, dims)
    q_s = jnp.transpose(q_cat[:, dims.n_prompt:], (1, 0, 2))
    o_lat = dmla_attention(q_s, kb[dims.n_prompt:].reshape(dims.dec, 1, -1), jnp.transpose(cache, (0, 2, 1)), page_table, dims)
    o_s = mla_up(jnp.transpose(o_lat, (1, 0, 2)), wv)
    o = _join(o_p, o_s, dims)
    y, = dense(o, w_o.astype(BF16), _epi_residual, [(F32, w_o.shape[1], min(512, w_o.shape[1]))],
               extras=[(x_res, "tile")], name="mla_out")
    return y, rows


def _block_means(x, n_blocks):
    return jnp.sum(x.reshape(n_blocks, NSA_BLOCK, x.shape[1]), axis=1) * (1.0 / NSA_BLOCK)


def _rank_select(imp, blk, n_cols, topk):
    rank = jnp.zeros(imp.shape, jnp.int32)
    for n in range(n_cols):
        col = imp[:, n:n + 1]
        beats = (col > imp) | ((col == imp) & (blk > n))
        rank = rank + jnp.where(beats, 1, 0)
    return rank


def _pnsa_kernel(q_ref, gate_ref, kvf_ref, ksel_ref, vsel_ref, kwin_ref, vwin_ref, tiles_ref, tab_ref, o_ref,
                 qs_ref, kc_ref, vc_ref, imp_ref, oc_ref, selk_ref, m_s, l_s, acc_s, m_w, l_w, acc_w,
                 *, hg, n_groups, n_blocks, topk, scale):
    qb = pl.program_id(1)
    step = pl.program_id(2)
    rows = hg * QBLOCK
    hd = HEAD_DIM
    seq = kvf_ref.shape[0]
    t_abs = qb * QBLOCK + lax.broadcasted_iota(jnp.int32, (QBLOCK, 1), 0)
    _stack_heads(q_ref, qs_ref, hg)

    width = kc_ref.shape[0]

    @pl.when(step == 0)
    def _():
        means = _block_means(kvf_ref[...], n_blocks)
        if width > n_blocks:
            kc_ref[...] = jnp.zeros_like(kc_ref)
            vc_ref[...] = jnp.zeros_like(vc_ref)
        kc_ref[0:n_blocks, :] = means[:, :hd].astype(BF16)
        vc_ref[0:n_blocks, :] = means[:, hd:].astype(BF16)
        imp_ref[...] = jnp.zeros_like(imp_ref)

    @pl.when(step < n_groups)
    def _():
        blk = lax.broadcasted_iota(jnp.int32, (QBLOCK, width), 1)
        dist = t_abs - (blk * NSA_BLOCK + NSA_BLOCK - 1)
        bucket = _rel_bucket(dist)
        per_head = []
        for h in range(hg):
            bias_h = jnp.zeros((QBLOCK, width), F32)
            for b in range(REL_BUCKETS):
                bias_h = jnp.where(bucket == b, tab_ref[b, step * hg + h], bias_h)
            per_head.append(bias_h)
        s = _dot_nt(qs_ref[...], kc_ref[...]) * scale + jnp.concatenate(per_head, axis=0)
        s = s.reshape(hg, QBLOCK, width)
        mask = (dist >= 0)[None]
        s = jnp.where(mask, s, NEG_BIG)
        e = jnp.where(mask, jnp.exp(s - jnp.max(s, axis=-1, keepdims=True)), 0.0)
        p = e / jnp.maximum(jnp.sum(e, axis=-1, keepdims=True), 1e-30)
        imp_ref[...] += jnp.sum(p, axis=0)
        oc = _dot(p.reshape(rows, width).astype(BF16), vc_ref[...])
        oc_ref[pl.ds(pl.multiple_of(step * rows, rows), rows), :] = oc

    @pl.when(step == n_groups - 1)
    def _():
        blk = lax.broadcasted_iota(jnp.int32, (QBLOCK, width), 1)
        cur = t_abs // NSA_BLOCK
        forced = (blk == 0) | (blk == cur) | (blk == cur - 1)
        imp = jnp.where(forced, 2.0, imp_ref[...])
        imp = jnp.where(blk <= cur, imp, -1.0)
        rank = _rank_select(imp, blk, n_blocks, topk)
        sel = jnp.where((rank < topk) & (imp >= 0.0), 1.0, 0.0).astype(BF16)
        kb = lax.broadcasted_iota(jnp.int32, (width, seq), 1) // NSA_BLOCK
        nb = lax.broadcasted_iota(jnp.int32, (width, seq), 0)
        expand = jnp.where(kb == nb, 1.0, 0.0).astype(BF16)
        selk_ref[...] = _dot(sel, expand)

    @pl.when(step >= n_groups)
    def _():
        g = step - n_groups
        row0 = pl.multiple_of(g * rows, rows)
        rel = _causal_rel(QBLOCK)
        _softmax_init(m_s, l_s, acc_s)
        _softmax_init(m_w, l_w, acc_w)
        qs = qs_ref[...]

        def scores(kc, c):
            cls = jnp.minimum(qb - c, 2)
            s = _dot_nt(qs, kc) * scale + tiles_ref[cls, pl.ds(row0, rows), :]
            return s.reshape(hg, QBLOCK, QBLOCK)

        def sel_body(c, carry):
            off = pl.multiple_of(c * QBLOCK, QBLOCK)
            picked = selk_ref[:, pl.ds(off, QBLOCK)] > 0.5
            mask = (picked & (rel <= (qb - c) * QBLOCK))[None]
            s = jnp.where(mask, scores(ksel_ref[pl.ds(off, QBLOCK), :], c), NEG_BIG).reshape(rows, QBLOCK)
            _online_softmax_step(s, vsel_ref[pl.ds(off, QBLOCK), :], m_s, l_s, acc_s, mask=s > 0.5 * NEG_BIG)
            return carry

        lax.fori_loop(0, qb + 1, sel_body, 0)

        def win_body(c, carry):
            off = pl.multiple_of(c * QBLOCK, QBLOCK)
            dist = (qb - c) * QBLOCK - rel
            mask = ((dist >= 0) & (dist < NSA_WINDOW))[None]
            s = jnp.where(mask, scores(kwin_ref[pl.ds(off, QBLOCK), :], c), NEG_BIG).reshape(rows, QBLOCK)
            _online_softmax_step(s, vwin_ref[pl.ds(off, QBLOCK), :], m_w, l_w, acc_w, mask=s > 0.5 * NEG_BIG)
            return carry

        lax.fori_loop(jnp.maximum(qb - NSA_WINDOW // QBLOCK, 0), qb + 1, win_body, 0)
        o_c = oc_ref[pl.ds(row0, rows), :]
        o_s = acc_s[...] / l_s[...]
        o_w = acc_w[...] / l_w[...]
        gate = gate_ref[0]
        for h in range(hg):
            r = slice(h * QBLOCK, (h + 1) * QBLOCK)
            o = (gate[:, h:h + 1] * o_c[r] + gate[:, hg + h:hg + h + 1] * o_s[r]
                 + gate[:, 2 * hg + h:2 * hg + h + 1] * o_w[r])
            o_ref[:, h * hd:(h + 1) * hd] = o.astype(o_ref.dtype)


def pnsa_attention(q, gates_g, kv, kvb, tiles, rel_table, dims, hg=8):
    hd = HEAD_DIM
    hg = min(hg, N_HEADS)
    ng = N_HEADS // hg
    qbs, seq = dims.qblocks, dims.seq
    nb = seq // NSA_BLOCK
    nbw = -(-nb // V7X_LANES) * V7X_LANES
    rows = hg * QBLOCK
    kern = functools.partial(_pnsa_kernel, hg=hg, n_groups=ng, n_blocks=nb, topk=min(NSA_TOPK, nb), scale=hd ** -0.5)

    def chan(c):
        return pl.BlockSpec((seq, hd), lambda b, i, s: (b, c))

    return pl.pallas_call(
        kern, grid=(dims.batch, qbs, 2 * ng),
        in_specs=[pl.BlockSpec((QBLOCK, rows), lambda b, i, s: (b * qbs + i, s % ng)),
                  pl.BlockSpec((1, QBLOCK, 3 * hg), lambda b, i, s: (s % ng, b * qbs + i, 0)),
                  pl.BlockSpec((seq, 2 * hd), lambda b, i, s: (b, 0)),
                  chan(2), chan(3), chan(4), chan(5),
                  pl.BlockSpec(tiles.shape, lambda b, i, s: (0, 0, 0)),
                  pl.BlockSpec(memory_space=pltpu.SMEM)],
        out_specs=pl.BlockSpec((QBLOCK, rows), lambda b, i, s: (b * qbs + i, jnp.maximum(s - ng, 0))),
        out_shape=jax.ShapeDtypeStruct((dims.n_prompt, N_HEADS * hd), BF16),
        scratch_shapes=[pltpu.VMEM((rows, hd), BF16), pltpu.VMEM((nbw, hd), BF16), pltpu.VMEM((nbw, hd), BF16),
                        pltpu.VMEM((QBLOCK, nbw), F32), pltpu.VMEM((N_HEADS * QBLOCK, hd), F32),
                        pltpu.VMEM((QBLOCK, seq), F32),
                        pltpu.VMEM((rows, V7X_LANES), F32), pltpu.VMEM((rows, V7X_LANES), F32), pltpu.VMEM((rows, hd), F32),
                        pltpu.VMEM((rows, V7X_LANES), F32), pltpu.VMEM((rows, V7X_LANES), F32), pltpu.VMEM((rows, hd), F32)],
        compiler_params=_cparams(("parallel", "parallel", "arbitrary"), 48), name="nsa_prompt_attention",
    )(q, gates_g, kv, kvb, kvb, kvb, kvb, tiles, rel_table)


def _dnsa_compress_kernel(pt_ref, q_ref, tab_ref, *refs, pp, scale, past):
    page_refs = refs[:pp]
    oc_ref, imp_ref, kc_ref, vc_ref = refs[pp:]
    step = pl.program_id(1)
    hd = HEAD_DIM
    n_chan = 4
    n_new = pp * (page_refs[0].shape[1] // n_chan) // NSA_BLOCK
    row0 = pl.multiple_of(step * n_new, n_new)
    kc_ref[pl.ds(row0, n_new), :] = _block_means(_page_channel(page_refs, range(pp), n_chan, 0), n_new)
    vc_ref[pl.ds(row0, n_new), :] = _block_means(_page_channel(page_refs, range(pp), n_chan, 1), n_new)

    @pl.when(step == pl.num_programs(1) - 1)
    def _():
        n_blocks = kc_ref.shape[0]
        blk = lax.broadcasted_iota(jnp.int32, (1, n_blocks), 1)
        dist = past - (blk * NSA_BLOCK + NSA_BLOCK - 1)
        s = _dot_nt(q_ref[0], kc_ref[...].astype(BF16)) * scale + _bias_rows(dist, tab_ref[...])
        mask = dist >= 0
        s = jnp.where(mask, s, NEG_BIG)
        e = jnp.where(mask, jnp.exp(s - jnp.max(s, axis=-1, keepdims=True)), 0.0)
        p = e / jnp.maximum(jnp.sum(e, axis=-1, keepdims=True), 1e-30)
        imp_ref[0] = jnp.sum(p, axis=0, keepdims=True)
        oc_ref[0] = _dot(p.astype(BF16), vc_ref[...].astype(BF16))


def dnsa_compress(q_s, cache, page_table, tab_t, dims, pp=8):
    hd = HEAD_DIM
    n_pages = dims.n_pages
    pp = min(pp, n_pages)
    assert n_pages % pp == 0 and (pp * dims.page // NSA_BLOCK) % 8 == 0
    nb = dims.past // NSA_BLOCK

    def page_spec(i):
        return pl.BlockSpec((1,) + cache.shape[1:], lambda s, g, pt: (pt[s, g * pp + i], 0, 0))

    kern = functools.partial(_dnsa_compress_kernel, pp=pp, scale=hd ** -0.5, past=dims.past)
    grid_spec = pltpu.PrefetchScalarGridSpec(
        num_scalar_prefetch=1, grid=(dims.dec, n_pages // pp),
        in_specs=[pl.BlockSpec((1, N_HEADS, hd), lambda s, g, pt: (s, 0, 0)),
                  pl.BlockSpec(tab_t.shape, lambda s, g, pt: (0, 0))] + [page_spec(i) for i in range(pp)],
        out_specs=[pl.BlockSpec((1, N_HEADS, hd), lambda s, g, pt: (s, 0, 0)),
                   pl.BlockSpec((1, 1, nb), lambda s, g, pt: (s, 0, 0))],
        scratch_shapes=[pltpu.VMEM((nb, hd), F32), pltpu.VMEM((nb, hd), F32)])
    return pl.pallas_call(
        kern, grid_spec=grid_spec,
        out_shape=[jax.ShapeDtypeStruct((dims.dec, N_HEADS, hd), F32), jax.ShapeDtypeStruct((dims.dec, 1, nb), F32)],
        compiler_params=_cparams(("parallel", "arbitrary"), 32), name="nsa_decode_compress",
    )(page_table, q_s, tab_t, *([cache] * pp))


def _dnsa_select_kernel(imp_ref, idx_ref, *, n_past, topk):
    n_seq, width = imp_ref.shape
    blk = lax.broadcasted_iota(jnp.int32, (n_seq, width), 1)
    cur = n_past
    imp = jnp.where((blk == 0) | (blk == cur) | (blk == cur - 1), 2.0, imp_ref[...])
    imp = jnp.where(blk <= cur, imp, -1.0)

    def body(n, rank):
        col = jnp.sum(jnp.where(blk == n, imp, 0.0), axis=1, keepdims=True)
        beats = (col > imp) | ((col == imp) & (blk > n))
        return rank + jnp.where(beats, 1, 0)

    rank = lax.fori_loop(0, cur + 1, body, jnp.zeros((n_seq, width), jnp.int32))
    lane = lax.broadcasted_iota(jnp.int32, (n_seq, V7X_LANES), 1)
    out = jnp.full((n_seq, V7X_LANES), -1, jnp.int32)
    for j in range(topk):
        picked = jnp.sum(jnp.where((rank == j) & (imp >= 0.0), blk + 1, 0), axis=1, keepdims=True) - 1
        out = jnp.where(lane == j, picked, out)
    idx_ref[...] = out


def dnsa_select(imp, dims):
    n_seq, n_past = imp.shape
    width = -(-(n_past + 1) // V7X_LANES) * V7X_LANES
    imp = jnp.pad(imp, ((0, 0), (0, width - n_past)))
    kern = functools.partial(_dnsa_select_kernel, n_past=n_past, topk=min(NSA_TOPK, n_past + 1))
    return pl.pallas_call(
        kern, grid=(1,),
        in_specs=[pl.BlockSpec((n_seq, width), lambda i: (0, 0))],
        out_specs=pl.BlockSpec((n_seq, V7X_LANES), lambda i: (0, 0)),
        out_shape=jax.ShapeDtypeStruct((n_seq, V7X_LANES), jnp.int32),
        compiler_params=_cparams(("arbitrary",), 16), name="nsa_decode_select",
    )(imp)


def _dnsa_attend_kernel(pt_ref, idx_ref, q_ref, new_ref, oc_ref, gate_ref, win_ref, tab_ref, *refs,
                        topk, scale, past, n_past):
    blk_refs = refs[:topk]
    o_ref = refs[topk]
    s_id = pl.program_id(0)
    hd = HEAD_DIM
    q = q_ref[0]
    qf = q.astype(F32)
    new = new_ref[0].astype(F32)
    tab_t = tab_ref[...]
    bias_self = tab_t[:, 0:1]

    def one_key_scores(k_row):
        return jnp.sum(qf * k_row, axis=-1, keepdims=True) * scale + bias_self

    n_keys = topk * NSA_BLOCK
    slot = lax.broadcasted_iota(jnp.int32, (1, n_keys), 1) // NSA_BLOCK
    within = lax.broadcasted_iota(jnp.int32, (1, n_keys), 1) % NSA_BLOCK
    blk_of = jnp.full((1, n_keys), -1, jnp.int32)
    has_new = jnp.zeros((1, 1), jnp.int32)
    for j in range(topk):
        bj = idx_ref[s_id, j]
        blk_of = jnp.where(slot == j, bj, blk_of)
        has_new = jnp.maximum(has_new, jnp.where(bj == n_past, 1, 0))
    valid = (blk_of >= 0) & (blk_of < n_past)
    dist = past - (blk_of * NSA_BLOCK + within)
    ksel = _page_channel(blk_refs, range(topk), 4, 2).astype(BF16)
    vsel = _page_channel(blk_refs, range(topk), 4, 3).astype(BF16)
    s = _dot_nt(q, ksel) * scale + _bias_rows(jnp.where(valid, dist, 0), tab_t)
    s = jnp.where(valid, s, NEG_BIG)
    new_ok = has_new > 0
    s_new = jnp.where(new_ok, one_key_scores(new[:, 2 * hd:3 * hd]), NEG_BIG)
    m = jnp.maximum(jnp.max(s, axis=-1, keepdims=True), s_new)
    e = jnp.where(valid, jnp.exp(s - m), 0.0)
    e_new = jnp.where(new_ok, jnp.exp(s_new - m), 0.0)
    l = jnp.sum(e, axis=-1, keepdims=True) + e_new
    o_s = (_dot(e.astype(BF16), vsel) + e_new * new[:, 3 * hd:4 * hd]) / l

    n_win = win_ref.shape[1] // 2
    r = lax.broadcasted_iota(jnp.int32, (1, n_win), 1)
    dist_w = n_win - r
    valid_w = (dist_w < NSA_WINDOW) & (past - dist_w >= 0)
    k_win = win_ref[0, pl.ds(0, n_win, stride=2), :].astype(BF16)
    v_win = win_ref[0, pl.ds(1, n_win, stride=2), :].astype(BF16)
    s_w = _dot_nt(q, k_win) * scale + _bias_rows(dist_w, tab_t)
    s_w = jnp.where(valid_w, s_w, NEG_BIG)
    s_wn = one_key_scores(new[:, 4 * hd:5 * hd])
    m_w = jnp.maximum(jnp.max(s_w, axis=-1, keepdims=True), s_wn)
    e_w = jnp.where(valid_w, jnp.exp(s_w - m_w), 0.0)
    e_wn = jnp.exp(s_wn - m_w)
    l_w = jnp.sum(e_w, axis=-1, keepdims=True) + e_wn
    o_w = (_dot(e_w.astype(BF16), v_win) + e_wn * new[:, 5 * hd:6 * hd]) / l_w

    gate = gate_ref[0]
    o_ref[0] = gate[:, 0:1] * oc_ref[0] + gate[:, 1:2] * o_s + gate[:, 2:3] * o_w


def dnsa_attend(q_s, new_rows, o_c, gates_s, win, cache_blocks, page_table, idx, tab_t, dims):
    hd = HEAD_DIM
    n_past = dims.past // NSA_BLOCK
    topk = min(NSA_TOPK, n_past + 1)
    per_page = dims.page // NSA_BLOCK

    def blk_spec(j):
        def index(s, pt, ix):
            b = jnp.clip(ix[s, j], 0, n_past - 1)
            return (pt[s, b // per_page] * per_page + b % per_page, 0, 0)
        return pl.BlockSpec((1,) + cache_blocks.shape[1:], index)

    def per_seq(shape):
        return pl.BlockSpec((1,) + shape, lambda s, pt, ix: (s,) + (0,) * len(shape))

    kern = functools.partial(_dnsa_attend_kernel, topk=topk, scale=hd ** -0.5, past=dims.past, n_past=n_past)
    grid_spec = pltpu.PrefetchScalarGridSpec(
        num_scalar_prefetch=2, grid=(dims.dec,),
        in_specs=[per_seq((N_HEADS, hd)), per_seq((1, 6 * hd)), per_seq((N_HEADS, hd)), per_seq((N_HEADS, 3)),
                  per_seq(win.shape[1:]), pl.BlockSpec(tab_t.shape, lambda s, pt, ix: (0, 0))]
                 + [blk_spec(j) for j in range(topk)],
        out_specs=per_seq((N_HEADS, hd)))
    return pl.pallas_call(
        kern, grid_spec=grid_spec, out_shape=jax.ShapeDtypeStruct((dims.dec, N_HEADS, hd), F32),
        compiler_params=_cparams(("parallel",), 32), name="nsa_decode_attend",
    )(page_table, idx, q_s, new_rows, o_c, gates_s, win, tab_t, *([cache_blocks] * topk))


def nsa_mixer(h, x_res, dims, cache, win_state, page_table, rel_table, tiles, w_in, w_o, hg=8):
    hd = HEAD_DIM
    nq = N_HEADS * hd
    nkv = 6 * hd
    q, = dense(h, w_in[:, :nq].astype(BF16), _epi_bf16, [(BF16, nq, min(512, nq))], name="nsa_q")
    kv, kvb = dense(h, w_in[:, nq:nq + nkv].astype(BF16), _epi_f32_bf16, [(F32, nkv, nkv), (BF16, nkv, nkv)], tn=nkv, name="nsa_kv")
    ngate = 3 * N_HEADS
    gates, = dense(h, w_in[:, nq + nkv:].astype(BF16), _epi_sigmoid, [(F32, ngate, ngate)], tn=ngate, name="nsa_gates")
    m = h.shape[0]
    hg = min(hg, N_HEADS)
    gates_g = gates.reshape(m, 3, N_HEADS // hg, hg).transpose(2, 0, 1, 3).reshape(N_HEADS // hg, m, 3 * hg)
    tab_t = rel_table.T
    o_p = pnsa_attention(q, gates_g, kv, kvb, tiles, rel_table, dims, hg=hg)
    q_s = _sample_heads(q, dims, hd)
    o_c, imp = dnsa_compress(q_s, _flat_cache(cache, dims.page), page_table, tab_t, dims)
    idx = dnsa_select(imp.reshape(dims.dec, -1), dims)
    gates_s = gates[dims.n_prompt:].reshape(dims.dec, 3, N_HEADS).transpose(0, 2, 1)
    o_s = dnsa_attend(q_s, kvb[dims.n_prompt:].reshape(dims.dec, 1, nkv), o_c, gates_s,
                      _flat_cache(win_state, win_state.shape[1]), _flat_cache(cache, NSA_BLOCK), page_table, idx, tab_t, dims)
    o = _join(o_p, o_s, dims)
    y, = dense(o, w_o.astype(BF16), _epi_residual, [(F32, w_o.shape[1], min(512, w_o.shape[1]))],
               extras=[(x_res, "tile")], name="nsa_out")
    return y, kv


def kernel(x_prompt, x_sample, cache_nsa_kv, state_nsa_win, cache_mla, cache_diff_kv, cache_sb_kv, page_table,
           rel_table, norm_g, final_g, nsa_w_in, nsa_w_o, mla_w_in, mla_g_q, mla_g_kv, mla_w_uq, mla_w_ukv, mla_w_o,
           diff_w_in, diff_lambda, diff_g_sub, diff_w_o, sb_w_in, sb_w_o, ffn_w_in, ffn_w_out):
    batch, seq, d = x_prompt.shape
    dec = x_sample.shape[0]
    page = cache_sb_kv.shape[1]
    dims = _Dims(batch, seq, dec, page_table.shape[1] * page, page)
    x = jnp.concatenate([x_prompt.reshape(-1, d), x_sample.reshape(-1, d)], axis=0)
    assert x_sample.shape[1] == 1, "the sample group decodes one token per sequence"
    n_p = dims.n_prompt
    tiles = bias_tiles(rel_table)
    rows = {}
    delta = None
    for i in range(norm_g.shape[0]):
        if delta is None:
            h = rms_norm(x, norm_g[i, 0])
        else:
            x, h = add_rms_norm(x, delta, norm_g[i, 0])
        kind = i % 4
        if kind == 0:
            x, rows[0] = nsa_mixer(h, x, dims, cache_nsa_kv, state_nsa_win, page_table, rel_table, tiles, nsa_w_in, nsa_w_o)
        elif kind == 1:
            x, rows[1] = mla_mixer(h, x, dims, cache_mla, page_table, mla_w_in, mla_g_q, mla_g_kv, mla_w_uq, mla_w_ukv, mla_w_o)
        elif kind == 2:
            x, rows[2] = diff_mixer(h, x, dims, cache_diff_kv, page_table, rel_table, tiles, diff_w_in, diff_lambda,
                                    diff_g_sub, diff_w_o)
        else:
            x, rows[3] = sb_mixer(h, x, dims, cache_sb_kv, page_table, sb_w_in, sb_w_o)
        h = rms_norm(x, norm_g[i, 1])
        delta = ffn(h, ffn_w_in[i], ffn_w_out[i])
    out = add_final_norm(x, delta, final_g)

    def split(a, tail):
        return a[:n_p].reshape((batch, seq) + tail), a[n_p:].reshape((dec, 1) + tail)

    hd = HEAD_DIM
    y_p, y_s = split(out, (d,))
    nsa_kv_p, nsa_kv_s = split(rows[0][:, :4 * hd], (4, 1, hd))
    win_p, win_s = split(rows[0][:, 4 * hd:], (2, 1, hd))
    nsa_win_p = win_p[:, -NSA_WINDOW:]
    nsa_win_s = jnp.concatenate([state_nsa_win, win_s], axis=1)[:, -state_nsa_win.shape[1]:]
    mla_p, mla_s = split(rows[1], (rows[1].shape[1],))
    diff_p, diff_s = split(rows[2], (2, 1, 2 * DIFF_D))
    sb_p, sb_s = split(rows[3], (2, SB_KV, hd))
    return (y_p, y_s, nsa_kv_p, nsa_kv_s, nsa_win_p, nsa_win_s, mla_p, mla_s, diff_p, diff_s, sb_p, sb_s)
```
